```python
import jax, jax.numpy as jnp
from jax import lax
import numpy as np

D_MODEL = 2048
BATCH = 32
SEQ = 256
DEPTH = 4
DEC_BATCH = 8
DEC_SEQ = 2048
PAST_LEN = 512

GRID_W = 64
N_MIXERS = 3
EPS = 1e-6
HG_HEADS = D_MODEL // 128
HG_DK = D_MODEL // HG_HEADS
HG_DV = D_MODEL // HG_HEADS
HG_FDIM = HG_HEADS * HG_DK
HG_CHUNK = 32
CONV_W = 3
HEAD_DIM = 128
N_HEADS = D_MODEL // HEAD_DIM
N_KV_HEADS = N_HEADS // 4
GQA_GROUP = N_HEADS // N_KV_HEADS
Q_BLOCK = 128
ROPE_THETA = 10000.0
ROPE_AXIS_DIM = HEAD_DIM // 2
D_FF = ((8 * D_MODEL // 3 + 255) // 256) * 256
N_EXPERTS = 8
TOP_K = 2
D_FF_EXPERT = D_FF
N_A_LAYERS = (DEPTH + 2) // 3
N_B_LAYERS = (DEPTH + 1) // 3
N_C_LAYERS = DEPTH // 3
N_DENSE_LAYERS = (DEPTH + 1) // 2
N_MOE_LAYERS = DEPTH // 2

kernel_name = 'hybrid_dit_hgrn2_conv_gqa_step'

F32 = jnp.float32


def rms_norm(x, w):
    xf = x.astype(F32)
    y = xf * lax.rsqrt(jnp.mean(xf * xf, axis=-1, keepdims=True) + EPS)
    return (y * w.astype(F32)).astype(x.dtype)


def modulation(cond, w, b):
    m = jnp.dot(jax.nn.silu(cond), w) + b
    return jnp.split(m[:, None, :], 6, axis=-1)


def hgrn_lower_bounds(lb_logits):
    p = jax.nn.softmax(lb_logits.astype(F32), axis=0)
    cs = jnp.cumsum(p, axis=0)
    return cs - cs[:1]


def hgrn_chunk_scan(q, k, v, g, s0):
    b, t, h, _ = q.shape
    dv = v.shape[-1]
    n = t // HG_CHUNK

    def chunks(a):
        return a.reshape(b, n, HG_CHUNK, h, a.shape[-1]).transpose(1, 0, 3, 2, 4)

    causal = jnp.tril(jnp.ones((HG_CHUNK, HG_CHUNK), dtype=bool))[:, :, None]

    def step(s, inp):
        qc, kc, vc, gc = inp
        cum = jnp.cumsum(gc, axis=2)
        o_inter = jnp.einsum('bhcd,bhde->bhce', qc * jnp.exp(cum), s)
        diff = cum[:, :, :, None, :] - cum[:, :, None, :, :]
        decay = jnp.exp(jnp.where(causal, diff, -jnp.inf))
        scores = jnp.einsum('bhid,bhjd,bhijd->bhij', qc, kc, decay)
        o = o_inter + jnp.einsum('bhij,bhje->bhie', scores, vc)
        last = cum[:, :, -1:, :]
        s_new = jnp.exp(last[:, :, 0, :, None]) * s + jnp.einsum('bhcd,bhce->bhde', kc * jnp.exp(last - cum), vc)
        return s_new, o

    s_fin, o = lax.scan(step, s0, (chunks(q), chunks(k), chunks(v), chunks(g)))
    o = o.transpose(1, 0, 3, 2, 4).reshape(b, t, h, dv)
    return o, s_fin


def hgrn_mixer(h, w_in, lb, norm_w, w_out, s0):
    b, t, _ = h.shape
    p = jnp.dot(h, w_in)
    q, f_fwd, f_bwd, i_in, gate = jnp.split(
        p, [HG_FDIM, 2 * HG_FDIM, 3 * HG_FDIM, 3 * HG_FDIM + D_MODEL], axis=-1)

    def heads(a, d):
        return a.astype(F32).reshape(b, t, HG_HEADS, d)

    q = heads(jax.nn.silu(q), HG_DK)
    v = heads(i_in, HG_DV)

    def forget(f, lb_d):
        lb_d = lb_d.reshape(HG_HEADS, HG_DK)
        logf = jnp.logaddexp(jnp.log(lb_d), jnp.log1p(-lb_d) + jax.nn.log_sigmoid(heads(f, HG_DK)))
        return -jnp.expm1(logf), logf

    k_f, g_f = forget(f_fwd, lb[0])
    k_b, g_b = forget(f_bwd, lb[1])
    s0 = s0.astype(F32)
    rev = lambda a: jnp.flip(a, axis=1)
    o_f, s_f = hgrn_chunk_scan(q, k_f, v, g_f, s0[:, 0])
    o_b, s_b = hgrn_chunk_scan(rev(q), rev(k_b), rev(v), rev(g_b), s0[:, 1])
    o = rms_norm(o_f + rev(o_b), norm_w).reshape(b, t, D_MODEL).astype(h.dtype)
    out = jnp.dot(o * jax.nn.silu(gate), w_out)
    return out, jnp.stack([s_f, s_b], axis=1).astype(h.dtype)


def shortconv_mixer(h, w_in, conv_w, w_out):
    b_gate, c_gate, xin = jnp.split(jnp.dot(h, w_in), 3, axis=-1)
    u = c_gate * xin
    y = lax.conv_general_dilated(
        u, conv_w[:, None, :].astype(u.dtype), window_strides=(1,),
        padding=((CONV_W // 2, CONV_W // 2),), dimension_numbers=('NWC', 'WIO', 'NWC'),
        feature_group_count=D_MODEL)
    return jnp.dot(b_gate * y, w_out)


def axial_rope(t):
    n_rows = t // GRID_W
    row = jnp.repeat(jnp.arange(n_rows), GRID_W).astype(F32)
    col = jnp.tile(jnp.arange(GRID_W), n_rows).astype(F32)
    inv = ROPE_THETA ** (-jnp.arange(0, ROPE_AXIS_DIM, 2, dtype=F32) / ROPE_AXIS_DIM)
    ang_r = row[:, None] * inv
    ang_c = col[:, None] * inv
    ang = jnp.concatenate([ang_r, ang_r, ang_c, ang_c], axis=-1)[:, None, :]
    return jnp.cos(ang), jnp.sin(ang)


def rotate_half(u):
    u1, u2 = jnp.split(u, 2, axis=-1)
    return jnp.concatenate([-u2, u1], axis=-1)


def apply_rope(x, cos, sin):
    xr, xc = jnp.split(x, 2, axis=-1)
    rot = jnp.concatenate([rotate_half(xr), rotate_half(xc)], axis=-1)
    return (x * cos + rot * sin).astype(x.dtype)


def attn_qkv(h, w_qkv, q_norm, k_norm):
    b, t, _ = h.shape
    q, k, v = jnp.split(jnp.dot(h, w_qkv), [N_HEADS * HEAD_DIM, (N_HEADS + N_KV_HEADS) * HEAD_DIM], axis=-1)
    q = rms_norm(q.reshape(b, t, N_HEADS, HEAD_DIM), q_norm)
    k = rms_norm(k.reshape(b, t, N_KV_HEADS, HEAD_DIM), k_norm)
    v = v.reshape(b, t, N_KV_HEADS, HEAD_DIM)
    return q, k, v


def block_attention(q, k, v):
    b, tq, _, _ = q.shape
    nb = tq // Q_BLOCK
    qb = q.reshape(b, nb, Q_BLOCK, N_KV_HEADS, GQA_GROUP, HEAD_DIM).transpose(1, 0, 2, 3, 4, 5)
    scale = HEAD_DIM ** -0.5

    def one_block(qblk):
        s = jnp.einsum('bqkgd,bskd->bkgqs', qblk, k, preferred_element_type=F32) * scale
        p = jax.nn.softmax(s, axis=-1).astype(v.dtype)
        return jnp.einsum('bkgqs,bskd->bqkgd', p, v)

    o = lax.map(one_block, qb)
    return o.transpose(1, 0, 2, 3, 4, 5).reshape(b, tq, N_HEADS * HEAD_DIM)


def swiglu(h, w_gu, w_down):
    a, g = jnp.split(jnp.dot(h, w_gu), 2, axis=-1)
    return jnp.dot(jax.nn.silu(a) * g, w_down)


def moe_swiglu(h, w_router, w_gu, w_down):
    b, t, d = h.shape
    x = h.reshape(b * t, d)
    logits = jnp.dot(x, w_router, preferred_element_type=F32)
    top_v, top_i = lax.top_k(logits, TOP_K)
    gates = jax.nn.softmax(top_v, axis=-1)
    combine = jnp.sum(jax.nn.one_hot(top_i, N_EXPERTS, dtype=F32) * gates[..., None], axis=1).astype(x.dtype)
    y = jnp.zeros_like(x)
    for e in range(N_EXPERTS):
        y = y + combine[:, e:e + 1] * swiglu(x, w_gu[e], w_down[e])
    return y.reshape(b, t, d)


def setup_inputs(seed: int = 0) -> dict:
    key = jax.random.key(seed)
    ks = jax.random.split(key, 32)

    def normal(k, shape, s=1.0):
        return jax.random.normal(k, shape, F32) * s

    def dense(k, shape, fan_in, s=1.0):
        return jax.random.normal(k, shape, F32) * (s * fan_in ** -0.5)

    def gain(k, shape):
        return 1.0 + 0.05 * jax.random.normal(k, shape, F32)

    return {
        'x_prompt': normal(ks[0], (BATCH, SEQ, D_MODEL)),
        'x_sample': normal(ks[1], (DEC_BATCH, DEC_SEQ, D_MODEL)),
        'state_hgrn': normal(ks[2], (DEC_BATCH, N_A_LAYERS, 2, HG_HEADS, HG_DK, HG_DV), 0.5),
        'cache_k': normal(ks[3], (DEC_BATCH, N_C_LAYERS, PAST_LEN, N_KV_HEADS, HEAD_DIM)),
        'cache_v': normal(ks[4], (DEC_BATCH, N_C_LAYERS, PAST_LEN, N_KV_HEADS, HEAD_DIM)),
        'c': normal(ks[5], (DEC_BATCH, D_MODEL)),
        'c_ctx': normal(ks[6], (D_MODEL,)),
        'ada_w': dense(ks[7], (DEPTH, D_MODEL, 6 * D_MODEL), D_MODEL, 0.5),
        'ada_b': normal(ks[8], (DEPTH, 6 * D_MODEL), 0.02),
        'norm_w': gain(ks[9], (DEPTH, 2, D_MODEL)),
        'final_norm_w': gain(ks[10], (D_MODEL,)),
        'hgrn_w_in': dense(ks[11], (N_A_LAYERS, D_MODEL, 3 * HG_FDIM + 2 * D_MODEL), D_MODEL),
        'hgrn_lb_logits': normal(ks[12], (N_A_LAYERS, 2, HG_FDIM)),
        'hgrn_norm_w': gain(ks[13], (N_A_LAYERS, HG_DV)),
        'hgrn_w_out': dense(ks[14], (N_A_LAYERS, D_MODEL, D_MODEL), D_MODEL),
        'conv_w_in': dense(ks[15], (N_B_LAYERS, D_MODEL, 3 * D_MODEL), D_MODEL),
        'conv_w': dense(ks[16], (N_B_LAYERS, CONV_W, D_MODEL), CONV_W),
        'conv_w_out': dense(ks[17], (N_B_LAYERS, D_MODEL, D_MODEL), D_MODEL),
        'attn_w_qkv': dense(ks[18], (N_C_LAYERS, D_MODEL, (N_HEADS + 2 * N_KV_HEADS) * HEAD_DIM), D_MODEL),
        'attn_q_norm': gain(ks[19], (N_C_LAYERS, HEAD_DIM)),
        'attn_k_norm': gain(ks[20], (N_C_LAYERS, HEAD_DIM)),
        'attn_w_out': dense(ks[21], (N_C_LAYERS, N_HEADS * HEAD_DIM, D_MODEL), N_HEADS * HEAD_DIM),
        'ffn_w_gate_up': dense(ks[22], (N_DENSE_LAYERS, D_MODEL, 2 * D_FF), D_MODEL),
        'ffn_w_down': dense(ks[23], (N_DENSE_LAYERS, D_FF, D_MODEL), D_FF),
        'moe_w_router': dense(ks[24], (N_MOE_LAYERS, D_MODEL, N_EXPERTS), D_MODEL),
        'moe_w_gate_up': dense(ks[25], (N_MOE_LAYERS, N_EXPERTS, D_MODEL, 2 * D_FF_EXPERT), D_MODEL),
        'moe_w_down': dense(ks[26], (N_MOE_LAYERS, N_EXPERTS, D_FF_EXPERT, D_MODEL), D_FF_EXPERT),
    }


def reference(x_prompt, x_sample, state_hgrn, cache_k, cache_v, c, c_ctx,
              ada_w, ada_b, norm_w, final_norm_w,
              hgrn_w_in, hgrn_lb_logits, hgrn_norm_w, hgrn_w_out,
              conv_w_in, conv_w, conv_w_out,
              attn_w_qkv, attn_q_norm, attn_k_norm, attn_w_out,
              ffn_w_gate_up, ffn_w_down,
              moe_w_router, moe_w_gate_up, moe_w_down):
    lower_bounds = hgrn_lower_bounds(hgrn_lb_logits)
    xp, xs = x_prompt, x_sample
    ctx_zero_state = jnp.zeros((xp.shape[0], 2, HG_HEADS, HG_DK, HG_DV), F32)
    rope_cos, rope_sin = axial_rope(xs.shape[1])
    new_states, new_k, new_v = [], [], []
    for layer in range(DEPTH):
        kind = layer % N_MIXERS
        j = layer // N_MIXERS
        sh1p, sc1p, g1p, sh2p, sc2p, g2p = modulation(c_ctx[None, :], ada_w[layer], ada_b[layer])
        sh1s, sc1s, g1s, sh2s, sc2s, g2s = modulation(c, ada_w[layer], ada_b[layer])
        hp = rms_norm(xp, norm_w[layer, 0]) * (1 + sc1p) + sh1p
        hs = rms_norm(xs, norm_w[layer, 0]) * (1 + sc1s) + sh1s
        if kind == 0:
            mp, st = hgrn_mixer(hp, hgrn_w_in[j], lower_bounds[j], hgrn_norm_w[j], hgrn_w_out[j], ctx_zero_state)
            ms, _ = hgrn_mixer(hs, hgrn_w_in[j], lower_bounds[j], hgrn_norm_w[j], hgrn_w_out[j], state_hgrn[:, j])
            new_states.append(st)
        elif kind == 1:
            mp = shortconv_mixer(hp, conv_w_in[j], conv_w[j], conv_w_out[j])
            ms = shortconv_mixer(hs, conv_w_in[j], conv_w[j], conv_w_out[j])
        else:
            qp, kp, vp = attn_qkv(hp, attn_w_qkv[j], attn_q_norm[j], attn_k_norm[j])
            mp = jnp.dot(block_attention(qp, kp, vp), attn_w_out[j])
            new_k.append(kp)
            new_v.append(vp)
            qs, ks_, vs = attn_qkv(hs, attn_w_qkv[j], attn_q_norm[j], attn_k_norm[j])
            qs = apply_rope(qs, rope_cos, rope_sin)
            ks_ = apply_rope(ks_, rope_cos, rope_sin)
            keys = jnp.concatenate([cache_k[:, j].astype(ks_.dtype), ks_], axis=1)
            vals = jnp.concatenate([cache_v[:, j].astype(vs.dtype), vs], axis=1)
            ms = jnp.dot(block_attention(qs, keys, vals), attn_w_out[j])
        xp = xp + g1p * mp
        xs = xs + g1s * ms
        hp = rms_norm(xp, norm_w[layer, 1]) * (1 + sc2p) + sh2p
        hs = rms_norm(xs, norm_w[layer, 1]) * (1 + sc2s) + sh2s
        f = layer // 2
        if layer % 2 == 0:
            fp = swiglu(hp, ffn_w_gate_up[f], ffn_w_down[f])
            fs = swiglu(hs, ffn_w_gate_up[f], ffn_w_down[f])
        else:
            fp = moe_swiglu(hp, moe_w_router[f], moe_w_gate_up[f], moe_w_down[f])
            fs = moe_swiglu(hs, moe_w_router[f], moe_w_gate_up[f], moe_w_down[f])
        xp = xp + g2p * fp
        xs = xs + g2s * fs
    y_prompt = rms_norm(xp, final_norm_w)
    y_sample = rms_norm(xs, final_norm_w)
    new_state_hgrn = jnp.stack(new_states, axis=1)
    new_cache_k = jnp.stack(new_k, axis=1)
    new_cache_v = jnp.stack(new_v, axis=1)
    return (y_prompt, y_sample, new_state_hgrn, new_cache_k, new_cache_v)
```

```python
import functools
import math

import numpy as np
import jax
import jax.numpy as jnp
from jax import lax
from jax.experimental import pallas as pl
from jax.experimental.pallas import tpu as pltpu

F32 = jnp.float32
BF16 = jnp.bfloat16

D_MODEL = 2048
BATCH = 32
SEQ = 256
DEPTH = 4
DEC_BATCH = 8
DEC_SEQ = 2048
PAST_LEN = 512
GRID_W = 64
N_MIXERS = 3
EPS = 1e-6
HG_HEADS = 16
HG_DK = 128
HG_DV = 128
HG_FDIM = HG_HEADS * HG_DK
CONV_W = 3
HEAD_DIM = 128
N_HEADS = 16
N_KV_HEADS = 4
GQA_GROUP = 4
ROPE_THETA = 10000.0
ROPE_AXIS_DIM = HEAD_DIM // 2
D_FF = 5632
N_EXPERTS = 8
TOP_K = 2

N_CTX = BATCH * SEQ
N_LAT = DEC_BATCH * DEC_SEQ
N_TOK = N_CTX + N_LAT
N_COND = 1 + DEC_BATCH
COND_PAD = 16

V7X_LANES = 128
V7X_VMEM_BYTES = 64 * 1024 * 1024
VMEM_CAP = 56 * 1024 * 1024

HG_CHUNK = 64
HG_LEVELS = int(math.log2(HG_CHUNK))
HG_TOT_ROWS = 16


def _cparams(sem, vmem_bytes):
    return pltpu.CompilerParams(dimension_semantics=sem,
                                vmem_limit_bytes=int(min(VMEM_CAP, max(vmem_bytes, 16 * 1024 * 1024))))


def _cond_of_tile(i, tm):
    r = i * tm
    return jnp.where(r < N_CTX, 0, 1 + (r - N_CTX) // DEC_SEQ)


def _silu(x):
    return x * jax.nn.sigmoid(x)


def _dot(a, b):
    return jnp.dot(a, b, preferred_element_type=F32)


def _dot_nt(a, b):
    return lax.dot_general(a, b, (((1,), (1,)), ((), ())), preferred_element_type=F32)


def _dot_tn(a, b):
    return lax.dot_general(a, b, (((0,), (0,)), ((), ())), preferred_element_type=F32)


def _mod_kernel(c_ref, w_ref, b_ref, o_ref):
    a = _silu(c_ref[...]).astype(BF16)
    o_ref[...] = _dot(a, w_ref[...].astype(BF16)) + b_ref[...]


def _modulation(conds, ada_w, ada_b):
    tn = 1024
    n_out = 6 * D_MODEL
    return pl.pallas_call(
        _mod_kernel,
        grid=(DEPTH, n_out // tn),
        in_specs=[pl.BlockSpec((COND_PAD, D_MODEL), lambda l, j: (0, 0)),
                  pl.BlockSpec((None, D_MODEL, tn), lambda l, j: (l, 0, j)),
                  pl.BlockSpec((None, 1, tn), lambda l, j: (l, 0, j))],
        out_specs=pl.BlockSpec((None, COND_PAD, tn), lambda l, j: (l, 0, j)),
        out_shape=jax.ShapeDtypeStruct((DEPTH, COND_PAD, n_out), F32),
        compiler_params=_cparams(("parallel", "parallel"), 3 * D_MODEL * tn * 4 + (4 << 20)),
        name="modulation",
    )(conds, ada_w, ada_b.reshape(DEPTH, 1, n_out))


def _norm_body(x_ref, w_ref, sc_ref, sh_ref):
    x = x_ref[...]
    y = x * lax.rsqrt(jnp.mean(x * x, axis=-1, keepdims=True) + EPS) * w_ref[...]
    return y * (1.0 + sc_ref[...]) + sh_ref[...]


def _norm_mod_kernel(x_ref, w_ref, sc_ref, sh_ref, o_ref):
    o_ref[...] = _norm_body(x_ref, w_ref, sc_ref, sh_ref).astype(o_ref.dtype)


def _final_norm_kernel(x_ref, w_ref, o_ref):
    x = x_ref[...]
    o_ref[...] = x * lax.rsqrt(jnp.mean(x * x, axis=-1, keepdims=True) + EPS) * w_ref[...]


def _norm_router_kernel(x_ref, w_ref, sc_ref, sh_ref, rh_ref, rl_ref, h_ref, r_ref):
    h = _norm_body(x_ref, w_ref, sc_ref, sh_ref)
    h_ref[...] = h
    hh = h.astype(BF16)
    hl = (h - hh.astype(F32)).astype(BF16)
    logits = _dot(hh, rh_ref[...]) + (_dot(hh, rl_ref[...]) + _dot(hl, rh_ref[...]))
    lane = lax.broadcasted_iota(jnp.int32, logits.shape, 1)
    neg = jnp.float32(-jnp.inf)
    logits = jnp.where(lane < N_EXPERTS, logits, neg)
    v1 = jnp.max(logits, axis=-1, keepdims=True)
    i1 = jnp.min(jnp.where(logits == v1, lane, V7X_LANES), axis=-1, keepdims=True)
    rest = jnp.where(lane == i1, neg, logits)
    v2 = jnp.max(rest, axis=-1, keepdims=True)
    i2 = jnp.min(jnp.where(rest == v2, lane, V7X_LANES), axis=-1, keepdims=True)
    e = jnp.exp(v2 - v1)
    g1 = 1.0 / (1.0 + e)
    g2 = e * g1
    out = jnp.where(lane == 0, i1.astype(F32), 0.0)
    out = jnp.where(lane == 1, i2.astype(F32), out)
    out = jnp.where(lane == 2, g1, out)
    out = jnp.where(lane == 3, g2, out)
    r_ref[...] = out


_NORM_TM = 512


def _row_specs(tm):
    x_spec = pl.BlockSpec((tm, D_MODEL), lambda i: (i, 0))
    w_spec = pl.BlockSpec((1, D_MODEL), lambda i: (0, 0))
    c_spec = pl.BlockSpec((None, 1, D_MODEL), lambda i: (_cond_of_tile(i, tm), 0, 0))
    return x_spec, w_spec, c_spec


def _norm_mod(x, w, sc, sh, out_dtype=BF16):
    tm = _NORM_TM
    x_spec, w_spec, c_spec = _row_specs(tm)
    return pl.pallas_call(
        _norm_mod_kernel,
        grid=(N_TOK // tm,),
        in_specs=[x_spec, w_spec, c_spec, c_spec],
        out_specs=x_spec,
        out_shape=jax.ShapeDtypeStruct((N_TOK, D_MODEL), out_dtype),
        compiler_params=_cparams(("parallel",), 6 * tm * D_MODEL * 4),
        name="norm_mod",
    )(x, w.reshape(1, D_MODEL), sc, sh)


def _final_norm(x, w):
    tm = _NORM_TM
    x_spec, w_spec, _ = _row_specs(tm)
    return pl.pallas_call(
        _final_norm_kernel,
        grid=(N_TOK // tm,),
        in_specs=[x_spec, w_spec],
        out_specs=x_spec,
        out_shape=jax.ShapeDtypeStruct((N_TOK, D_MODEL), F32),
        compiler_params=_cparams(("parallel",), 6 * tm * D_MODEL * 4),
        name="final_norm",
    )(x, w.reshape(1, D_MODEL))


def _norm_router(x, w, sc, sh, w_router):
    tm = _NORM_TM
    x_spec, w_spec, c_spec = _row_specs(tm)
    wr = jnp.zeros((D_MODEL, V7X_LANES), F32).at[:, :N_EXPERTS].set(w_router)
    wr_hi = wr.astype(BF16)
    wr_lo = (wr - wr_hi.astype(F32)).astype(BF16)
    r_spec = pl.BlockSpec((D_MODEL, V7X_LANES), lambda i: (0, 0))
    return pl.pallas_call(
        _norm_router_kernel,
        grid=(N_TOK // tm,),
        in_specs=[x_spec, w_spec, c_spec, c_spec, r_spec, r_spec],
        out_specs=[x_spec, pl.BlockSpec((tm, V7X_LANES), lambda i: (i, 0))],
        out_shape=[jax.ShapeDtypeStruct((N_TOK, D_MODEL), F32),
                   jax.ShapeDtypeStruct((N_TOK, V7X_LANES), F32)],
        compiler_params=_cparams(("parallel",), 8 * tm * D_MODEL * 4),
        name="norm_router",
    )(x, w.reshape(1, D_MODEL), sc, sh, wr_hi, wr_lo)


def _mm_kernel(a_ref, w_ref, o_ref):
    o_ref[...] = _dot(a_ref[...], w_ref[...]).astype(o_ref.dtype)


def _mm_res_kernel(a_ref, w_ref, x_ref, g_ref, o_ref):
    o_ref[...] = x_ref[...] + g_ref[...] * _dot(a_ref[...], w_ref[...])


def _matmul(a, w, tm, tn, out_dtype=F32):
    m, k = a.shape
    n = w.shape[1]
    vm = 2 * (tm * k * 2 + k * tn * 2 + tm * tn * 4) + tm * tn * 4
    return pl.pallas_call(
        _mm_kernel,
        grid=(m // tm, n // tn),
        in_specs=[pl.BlockSpec((tm, k), lambda i, j: (i, 0)),
                  pl.BlockSpec((k, tn), lambda i, j: (0, j))],
        out_specs=pl.BlockSpec((tm, tn), lambda i, j: (i, j)),
        out_shape=jax.ShapeDtypeStruct((m, n), out_dtype),
        compiler_params=_cparams(("parallel", "parallel"), vm + (4 << 20)),
        name="matmul",
    )(a, w)


def _matmul_residual(a, w, x, gate, tm, tn):
    m, k = a.shape
    n = w.shape[1]
    vm = 2 * (tm * k * 2 + k * tn * 2 + 2 * tm * tn * 4) + tm * tn * 4
    return pl.pallas_call(
        _mm_res_kernel,
        grid=(m // tm, n // tn),
        in_specs=[pl.BlockSpec((tm, k), lambda i, j: (i, 0)),
                  pl.BlockSpec((k, tn), lambda i, j: (0, j)),
                  pl.BlockSpec((tm, tn), lambda i, j: (i, j)),
                  pl.BlockSpec((None, 1, tn), lambda i, j: (_cond_of_tile(i, tm), 0, j))],
        out_specs=pl.BlockSpec((tm, tn), lambda i, j: (i, j)),
        out_shape=jax.ShapeDtypeStruct((m, n), F32),
        compiler_params=_cparams(("parallel", "parallel"), vm + (4 << 20)),
        name="matmul_residual",
    )(a, w, x, gate)


_FFN_TF = 512


def _ffn_accumulate(nu_ref, x_ref, wa_ref, wg_ref, wd_ref, acc_ref, finish):
    i = pl.program_id(0)
    f = pl.program_id(1)

    @pl.when(i < nu_ref[0])
    def _():
        x = x_ref[...]
        a = _dot(x, wa_ref[...])
        g = _dot(x, wg_ref[...])
        hid = (_silu(a) * g).astype(BF16)
        part = _dot(hid, wd_ref[...])

        @pl.when(f == 0)
        def _():
            acc_ref[...] = part

        @pl.when(f > 0)
        def _():
            acc_ref[...] += part

        @pl.when(f == pl.num_programs(1) - 1)
        def _():
            finish(acc_ref[...])


def _ffn_moe_kernel(te_ref, nu_ref, x_ref, wa_ref, wg_ref, wd_ref, o_ref, acc_ref):
    def finish(acc):
        o_ref[...] = acc
    _ffn_accumulate(nu_ref, x_ref, wa_ref, wg_ref, wd_ref, acc_ref, finish)

    @pl.when((pl.program_id(0) >= nu_ref[0]) & (pl.program_id(1) == 0))
    def _():
        o_ref[...] = jnp.zeros(o_ref.shape, o_ref.dtype)


def _ffn_res_kernel(te_ref, nu_ref, x_ref, wa_ref, wg_ref, wd_ref, r_ref, g_ref, o_ref, acc_ref):
    def finish(acc):
        o_ref[...] = r_ref[...] + g_ref[...] * acc
    _ffn_accumulate(nu_ref, x_ref, wa_ref, wg_ref, wd_ref, acc_ref, finish)


def _ffn_call(xb, w_gu, w_down, tile_expert, n_used, tm, residual=None):
    rows = xb.shape[0]
    tf = _FFN_TF
    nf = D_FF // tf
    in_specs = [pl.BlockSpec((tm, D_MODEL), lambda i, f, te, nu: (i, 0)),
                pl.BlockSpec((None, D_MODEL, tf), lambda i, f, te, nu: (te[i], 0, f)),
                pl.BlockSpec((None, D_MODEL, tf), lambda i, f, te, nu: (te[i], 0, f + nf)),
                pl.BlockSpec((None, tf, D_MODEL), lambda i, f, te, nu: (te[i], f, 0))]
    args = [xb, w_gu, w_gu, w_down]
    if residual is None:
        body = _ffn_moe_kernel
    else:
        body = _ffn_res_kernel
        x, gate = residual
        in_specs += [pl.BlockSpec((tm, D_MODEL), lambda i, f, te, nu: (i, 0)),
                     pl.BlockSpec((None, 1, D_MODEL), lambda i, f, te, nu: (_cond_of_tile(i, tm), 0, 0))]
        args += [x, gate]
    vm = 2 * (tm * D_MODEL * 2 + 3 * D_MODEL * tf * 2 + 2 * tm * D_MODEL * 4) + tm * D_MODEL * 4 + 3 * tm * tf * 4
    return pl.pallas_call(
        body,
        grid_spec=pltpu.PrefetchScalarGridSpec(
            num_scalar_prefetch=2,
            grid=(rows // tm, nf),
            in_specs=in_specs,
            out_specs=pl.BlockSpec((tm, D_MODEL), lambda i, f, te, nu: (i, 0)),
            scratch_shapes=[pltpu.VMEM((tm, D_MODEL), F32)]),
        out_shape=jax.ShapeDtypeStruct((rows, D_MODEL), F32),
        compiler_params=_cparams(("parallel", "arbitrary"), vm + (4 << 20)),
        name="swiglu",
    )(tile_expert, n_used, *args)


def _dense_ffn(hb, w_gu, w_down, x, gate):
    tm = 512
    nt = N_TOK // tm
    return _ffn_call(hb, w_gu[None], w_down[None], jnp.zeros((nt,), jnp.int32),
                     jnp.full((1,), nt, jnp.int32), tm, residual=(x, gate))


_MOE_TM = 512
_MOE_ROWS = TOP_K * N_TOK + N_EXPERTS * _MOE_TM
_GATHER_TM = 256


def _gather_kernel(idx_ref, h_hbm, o_ref, buf, sem):
    i = pl.program_id(0)
    tm = o_ref.shape[0]

    def issue(tile, slot):
        def body(r, carry):
            src = h_hbm.at[pl.ds(idx_ref[tile * tm + r], 1), :]
            pltpu.make_async_copy(src, buf.at[slot, pl.ds(r, 1), :], sem.at[slot]).start()
            return carry
        lax.fori_loop(0, tm, body, 0)

    @pl.when(i == 0)
    def _():
        issue(0, 0)

    @pl.when(i + 1 < pl.num_programs(0))
    def _():
        issue(i + 1, (i + 1) % 2)

    slot = i % 2
    pltpu.make_async_copy(h_hbm.at[pl.ds(0, tm), :], buf.at[slot], sem.at[slot]).wait()
    o_ref[...] = buf[slot].astype(o_ref.dtype)


def _gather_rows(h, idx, rows):
    tm = _GATHER_TM
    return pl.pallas_call(
        _gather_kernel,
        grid_spec=pltpu.PrefetchScalarGridSpec(
            num_scalar_prefetch=1,
            grid=(rows // tm,),
            in_specs=[pl.BlockSpec(memory_space=pl.ANY)],
            out_specs=pl.BlockSpec((tm, D_MODEL), lambda i, idx: (i, 0)),
            scratch_shapes=[pltpu.VMEM((2, tm, D_MODEL), F32), pltpu.SemaphoreType.DMA((2,))]),
        out_shape=jax.ShapeDtypeStruct((rows, D_MODEL), BF16),
        compiler_params=_cparams(("arbitrary",), 6 * tm * D_MODEL * 4),
        name="moe_gather",
    )(idx, h)


def _combine_kernel(d0_ref, d1_ref, y_hbm, x_ref, gt_ref, w_ref, o_ref, buf, sem):
    i = pl.program_id(0)
    tm = o_ref.shape[0]

    def issue(tile, slot):
        def body(r, carry):
            t = tile * tm + r
            pltpu.make_async_copy(y_hbm.at[pl.ds(d0_ref[t], 1), :], buf.at[slot, 0, pl.ds(r, 1), :],
                                  sem.at[slot]).start()
            pltpu.make_async_copy(y_hbm.at[pl.ds(d1_ref[t], 1), :], buf.at[slot, 1, pl.ds(r, 1), :],
                                  sem.at[slot]).start()
            return carry
        lax.fori_loop(0, tm, body, 0)

    @pl.when(i == 0)
    def _():
        issue(0, 0)

    @pl.when(i + 1 < pl.num_programs(0))
    def _():
        issue(i + 1, (i + 1) % 2)

    slot = i % 2
    pltpu.make_async_copy(y_hbm.at[pl.ds(0, tm), :], buf.at[slot, 0], sem.at[slot]).wait()
    pltpu.make_async_copy(y_hbm.at[pl.ds(0, tm), :], buf.at[slot, 1], sem.at[slot]).wait()
    w = w_ref[...]
    y = w[:, 2:3] * buf[slot, 0] + w[:, 3:4] * buf[slot, 1]
    o_ref[...] = x_ref[...] + gt_ref[...] * y


def _combine_rows(y, d0, d1, x, gate, route):
    tm = _GATHER_TM
    return pl.pallas_call(
        _combine_kernel,
        grid_spec=pltpu.PrefetchScalarGridSpec(
            num_scalar_prefetch=2,
            grid=(N_TOK // tm,),
            in_specs=[pl.BlockSpec(memory_space=pl.ANY),
                      pl.BlockSpec((tm, D_MODEL), lambda i, a, b: (i, 0)),
                      pl.BlockSpec((None, 1, D_MODEL), lambda i, a, b: (_cond_of_tile(i, tm), 0, 0)),
                      pl.BlockSpec((tm, V7X_LANES), lambda i, a, b: (i, 0))],
            out_specs=pl.BlockSpec((tm, D_MODEL), lambda i, a, b: (i, 0)),
            scratch_shapes=[pltpu.VMEM((2, 2, tm, D_MODEL), F32), pltpu.SemaphoreType.DMA((2,))]),
        out_shape=jax.ShapeDtypeStruct((N_TOK, D_MODEL), F32),
        compiler_params=_cparams(("arbitrary",), 10 * tm * D_MODEL * 4),
        name="moe_combine",
    )(d0, d1, y, x, gate, route)


def _moe_ffn(x, norm_w, sc, sh, gate, w_router, w_gu, w_down):
    h, route = _norm_router(x, norm_w, sc, sh, w_router)
    top_i = route[:, :TOP_K].astype(jnp.int32)
    member = jnp.sum(top_i[:, :, None] == jnp.arange(N_EXPERTS, dtype=jnp.int32), axis=1, dtype=jnp.int32)
    before = jnp.cumsum(member, axis=0) - member
    counts = before[-1] + member[-1]
    padded = ((counts + _MOE_TM - 1) // _MOE_TM) * _MOE_TM
    ends = jnp.cumsum(padded)
    starts = ends - padded
    dest = jnp.take(starts, top_i) + jnp.take_along_axis(before, top_i, axis=1)
    tok = jnp.broadcast_to(jnp.arange(N_TOK, dtype=jnp.int32)[:, None], (N_TOK, TOP_K))
    row_tok = jnp.zeros((_MOE_ROWS,), jnp.int32).at[dest.reshape(-1)].set(tok.reshape(-1))
    n_tiles = _MOE_ROWS // _MOE_TM
    tile_start = jnp.arange(n_tiles, dtype=jnp.int32) * _MOE_TM
    tile_expert = jnp.minimum(jnp.sum(tile_start[:, None] >= ends[None, :], axis=1), N_EXPERTS - 1).astype(jnp.int32)
    n_used = (ends[-1:] // _MOE_TM).astype(jnp.int32)
    xs = _gather_rows(h, row_tok, _MOE_ROWS)
    ys = _ffn_call(xs, w_gu, w_down, tile_expert, n_used, _MOE_TM)
    return _combine_rows(ys, dest[:, 0], dest[:, 1], x, gate, route)


@functools.lru_cache(maxsize=None)
def _hgrn_tables():
    c = HG_CHUNK
    n_rows = c * (HG_LEVELS + 1) + HG_TOT_ROWS
    prefix = np.zeros((2, n_rows, c), np.float32)
    role = np.zeros((2, HG_LEVELS, c, 1), np.float32)
    pair = np.zeros((2, HG_LEVELS, c, c), np.float32)
    for d in range(2):
        for i in range(c):
            if d == 0:
                prefix[d, i, :i + 1] = 1.0
            else:
                prefix[d, i, i:] = 1.0
        for l in range(HG_LEVELS):
            m = 2 ** l
            for i in range(c):
                blk = i // (2 * m)
                upper = (i // m) % 2 == 1
                mid = blk * 2 * m + m
                row = c * (l + 1) + i
                if d == 0:
                    is_q = upper
                    lo, hi = (mid, i + 1) if upper else (i + 1, mid)
                else:
                    is_q = not upper
                    lo, hi = (mid, i) if upper else (i, mid)
                prefix[d, row, lo:hi] = 1.0
                role[d, l, i, 0] = 1.0 if is_q else 0.0
            for i in range(c):
                for j in range(c):
                    same = i // (2 * m) == j // (2 * m)
                    if same and role[d, l, i, 0] == 1.0 and role[d, l, j, 0] == 0.0:
                        pair[d, l, i, j] = 1.0
        prefix[d, c * (HG_LEVELS + 1):, :] = 1.0
    role = np.broadcast_to(role, (2, HG_LEVELS, c, HG_DK)).copy()
    return prefix, role, pair


def _hgrn_kernel(*refs, seq, hb, has_init, write_state):
    it = iter(refs)
    q_ref, ff_ref, fb_ref, v_ref, gt_ref = next(it), next(it), next(it), next(it), next(it)
    la_ref, l1_ref, nw_ref, pm_ref, role_ref, pair_ref = (next(it) for _ in range(6))
    s0_ref = next(it) if has_init else None
    o_ref = next(it)
    st_ref = next(it) if write_state else None
    of_scr, ob_scr, stt_scr = next(it), next(it), next(it)

    c = HG_CHUNK
    n_chunk = seq // c
    ones = jnp.ones((HG_DK, HG_DK), BF16)

    for d in range(2):
        for h in range(hb):
            if has_init:
                stt_scr[d, h] = s0_ref[d, h].T
            else:
                stt_scr[d, h] = jnp.zeros((HG_DV, HG_DK), F32)

    def chunk_unit(d, h, ci):
        r0 = pl.multiple_of(ci * c, c)
        cols = slice(h * HG_DK, (h + 1) * HG_DK)
        qp = q_ref[pl.ds(r0, c), cols]
        qh = _silu(qp)
        z = (ff_ref if d == 0 else fb_ref)[pl.ds(r0, c), cols]
        v = v_ref[pl.ds(r0, c), cols]
        la = la_ref[d, :, cols]
        l1 = l1_ref[d, :, cols]
        b = l1 + (jnp.minimum(z, 0.0) - jnp.log1p(jnp.exp(-jnp.abs(z))))
        g = jnp.maximum(la, b) + jnp.log1p(jnp.exp(-jnp.abs(la - b)))
        k = 1.0 - jnp.exp(g)
        g1 = g.astype(BF16)
        rem = g - g1.astype(F32)
        g2 = rem.astype(BF16)
        g3 = (rem - g2.astype(F32)).astype(BF16)
        pm = pm_ref[d]
        sums = _dot(pm, g1) + (_dot(pm, g2) + _dot(pm, g3))
        cum = sums[0:c]
        tot = sums[c * (HG_LEVELS + 1):c * (HG_LEVELS + 1) + 1]
        scores = jnp.zeros((c, c), F32)
        for l in range(HG_LEVELS):
            e = jnp.exp(sums[c * (l + 1):c * (l + 2)])
            xk = (jnp.where(role_ref[d, l] > 0.5, qh, k) * e).astype(BF16)
            scores = scores + pair_ref[d, l] * _dot_nt(xk, xk)
        self_score = _dot((qh * k).astype(BF16), ones)
        vb = v.astype(BF16)
        o = _dot(scores.astype(BF16), vb) + self_score * v
        stt = stt_scr[d, h]
        o = o + _dot_nt((qh * jnp.exp(cum)).astype(BF16), stt.astype(BF16))
        kt = (k * jnp.exp(tot - cum)).astype(BF16)
        stt_scr[d, h] = stt * jnp.exp(tot) + _dot_tn(vb, kt)
        (of_scr if d == 0 else ob_scr)[pl.ds(r0, c), cols] = o

    def scan_body(ci, carry):
        for h in range(hb):
            chunk_unit(0, h, ci)
            chunk_unit(1, h, n_chunk - 1 - ci)
        return carry

    lax.fori_loop(0, n_chunk, scan_body, 0)

    def out_body(ci, carry):
        r0 = pl.multiple_of(ci * c, c)
        for h in range(hb):
            cols = slice(h * HG_DV, (h + 1) * HG_DV)
            o = of_scr[pl.ds(r0, c), cols] + ob_scr[pl.ds(r0, c), cols]
            y = o * lax.rsqrt(jnp.mean(o * o, axis=-1, keepdims=True) + EPS) * nw_ref[...]
            o_ref[pl.ds(r0, c), cols] = (y * _silu(gt_ref[pl.ds(r0, c), cols])).astype(o_ref.dtype)
        return carry

    lax.fori_loop(0, n_chunk, out_body, 0)

    if write_state:
        for d in range(2):
            for h in range(hb):
                st_ref[d, h] = stt_scr[d, h].T


def _hgrn_scan(p, log_lb, log1m_lb, norm_w, rows0, seq, nb, hb, s0=None, write_state=False):
    w = hb * HG_DK
    ngrp = HG_HEADS // hb
    per = D_MODEL // w
    rb0 = rows0 // seq
    prefix, role, pair = _hgrn_tables()

    def col(group):
        return pl.BlockSpec((seq, w), lambda b, g: (rb0 + b, group * per + g))

    in_specs = [col(0), col(1), col(2), col(3), col(4),
                pl.BlockSpec((2, 1, w), lambda b, g: (0, 0, g)),
                pl.BlockSpec((2, 1, w), lambda b, g: (0, 0, g)),
                pl.BlockSpec((1, HG_DV), lambda b, g: (0, 0)),
                pl.BlockSpec(prefix.shape, lambda b, g: (0, 0, 0)),
                pl.BlockSpec(role.shape, lambda b, g: (0, 0, 0, 0)),
                pl.BlockSpec(pair.shape, lambda b, g: (0, 0, 0, 0))]
    args = [p, p, p, p, p, log_lb, log1m_lb, norm_w.reshape(1, HG_DV),
            jnp.asarray(prefix, BF16), jnp.asarray(role), jnp.asarray(pair)]
    if s0 is not None:
        in_specs.append(pl.BlockSpec((None, 2, hb, HG_DK, HG_DV), lambda b, g: (b, 0, g, 0, 0)))
        args.append(s0)
    out_specs = [pl.BlockSpec((seq, w), lambda b, g: (b, g))]
    out_shape = [jax.ShapeDtypeStruct((nb * seq, D_MODEL), BF16)]
    if write_state:
        out_specs.append(pl.BlockSpec((None, 2, hb, HG_DK, HG_DV), lambda b, g: (b, 0, g, 0, 0)))
        out_shape.append(jax.ShapeDtypeStruct((nb, 2, HG_HEADS, HG_DK, HG_DV), F32))
    vm = 2 * 5 * seq * w * 4 + 2 * seq * w * 2 + 2 * seq * w * 4 + 8 * hb * HG_DK * HG_DV * 4 + (8 << 20)
    res = pl.pallas_call(
        functools.partial(_hgrn_kernel, seq=seq, hb=hb, has_init=s0 is not None, write_state=write_state),
        grid=(nb, ngrp),
        in_specs=in_specs,
        out_specs=out_specs,
        out_shape=out_shape,
        scratch_shapes=[pltpu.VMEM((seq, w), F32), pltpu.VMEM((seq, w), F32),
                        pltpu.VMEM((2, hb, HG_DV, HG_DK), F32)],
        compiler_params=_cparams(("parallel", "parallel"), vm),
        name="hgrn_scan",
    )(*args)
    return res


def _hgrn_mixer(hbf, w_in, lb, norm_w, w_out, s0_lat, x, gate):
    p = _matmul(hbf, w_in, 1024, 512)
    log_lb = jnp.log(lb).reshape(2, 1, HG_FDIM)
    log1m_lb = jnp.log1p(-lb).reshape(2, 1, HG_FDIM)
    o_ctx, st = _hgrn_scan(p, log_lb, log1m_lb, norm_w, 0, SEQ, BATCH, 8, write_state=True)
    (o_lat,) = _hgrn_scan(p, log_lb, log1m_lb, norm_w, N_CTX, DEC_SEQ, DEC_BATCH, 2, s0=s0_lat)
    o = jnp.concatenate([o_ctx, o_lat], axis=0)
    return _matmul_residual(o, w_out, x, gate, 1024, 512), st


def _conv_kernel(b_ref, c_ref, x_ref, w_ref, o_ref):
    u = c_ref[...] * x_ref[...]
    seq = u.shape[0]
    row = lax.broadcasted_iota(jnp.int32, u.shape, 0)
    prev = jnp.where(row == 0, 0.0, pltpu.roll(u, 1, 0))
    nxt = jnp.where(row == seq - 1, 0.0, pltpu.roll(u, seq - 1, 0))
    w = w_ref[...]
    y = prev * w[0:1] + u * w[1:2] + nxt * w[2:3]
    o_ref[...] = (b_ref[...] * y).astype(o_ref.dtype)


def _conv_gate(p, conv_w, rows0, seq, nb):
    tc = 256
    per = D_MODEL // tc
    rb0 = rows0 // seq

    def col(group):
        return pl.BlockSpec((seq, tc), lambda b, j: (rb0 + b, group * per + j))

    return pl.pallas_call(
        _conv_kernel,
        grid=(nb, per),
        in_specs=[col(0), col(1), col(2), pl.BlockSpec((CONV_W, tc), lambda b, j: (0, j))],
        out_specs=pl.BlockSpec((seq, tc), lambda b, j: (b, j)),
        out_shape=jax.ShapeDtypeStruct((nb * seq, D_MODEL), BF16),
        compiler_params=_cparams(("parallel", "parallel"), 14 * seq * tc * 4),
        name="conv_gate",
    )(p, p, p, conv_w)


def _conv_mixer(hbf, w_in, conv_w, w_out, x, gate):
    p = _matmul(hbf, w_in, 1024, 512)
    o = jnp.concatenate([_conv_gate(p, conv_w, 0, SEQ, BATCH),
                         _conv_gate(p, conv_w, N_CTX, DEC_SEQ, DEC_BATCH)], axis=0)
    return _matmul_residual(o, w_out, x, gate, 1024, 512)


def _head_norm(x, w):
    return x * lax.rsqrt(jnp.mean(x * x, axis=-1, keepdims=True) + EPS) * w


def _rope(x, cos, sin_signed):
    lane = lax.broadcasted_iota(jnp.int32, x.shape, 1)
    first = (lane % ROPE_AXIS_DIM) < (ROPE_AXIS_DIM // 2)
    rot = jnp.where(first, pltpu.roll(x, HEAD_DIM - ROPE_AXIS_DIM // 2, 1), pltpu.roll(x, ROPE_AXIS_DIM // 2, 1))
    return x * cos + rot * sin_signed


def _kprep_kernel(*refs, rope):
    if rope:
        k_ref, w_ref, cos_ref, sin_ref, kn_ref, kb_ref = refs
    else:
        k_ref, w_ref, kn_ref, kb_ref = refs
    for h in range(N_KV_HEADS):
        cols = slice(h * HEAD_DIM, (h + 1) * HEAD_DIM)
        kn = _head_norm(k_ref[:, cols], w_ref[...])
        kn_ref[:, cols] = kn
        if rope:
            kn = _rope(kn, cos_ref[...], sin_ref[...])
        kb_ref[:, cols] = kn.astype(BF16)


def _k_prep(qkv, k_norm, rows0, nrows, rope_tabs):
    tm = 512
    kw = N_KV_HEADS * HEAD_DIM
    kcol = (N_HEADS * HEAD_DIM) // kw
    rb0 = rows0 // tm
    in_specs = [pl.BlockSpec((tm, kw), lambda i: (rb0 + i, kcol)),
                pl.BlockSpec((1, HEAD_DIM), lambda i: (0, 0))]
    args = [qkv, k_norm.reshape(1, HEAD_DIM)]
    if rope_tabs is not None:
        per = DEC_SEQ // tm
        in_specs += [pl.BlockSpec((tm, HEAD_DIM), lambda i: (i % per, 0))] * 2
        args += list(rope_tabs)
    o_spec = pl.BlockSpec((tm, kw), lambda i: (i, 0))
    return pl.pallas_call(
        functools.partial(_kprep_kernel, rope=rope_tabs is not None),
        grid=(nrows // tm,),
        in_specs=in_specs,
        out_specs=[o_spec, o_spec],
        out_shape=[jax.ShapeDtypeStruct((nrows, kw), F32), jax.ShapeDtypeStruct((nrows, kw), BF16)],
        compiler_params=_cparams(("parallel",), 12 * tm * kw * 4),
        name="k_prep",
    )(*args)


def _attn_kernel(*refs, rope):
    if rope:
        q_ref, k_ref, v_ref, w_ref, cos_ref, sin_ref, o_ref = refs
    else:
        q_ref, k_ref, v_ref, w_ref, o_ref = refs
    kk = k_ref[...]
    vv = v_ref[...]
    scale = HEAD_DIM ** -0.5
    for g in range(GQA_GROUP):
        cols = slice(g * HEAD_DIM, (g + 1) * HEAD_DIM)
        q = _head_norm(q_ref[:, cols], w_ref[...])
        if rope:
            q = _rope(q, cos_ref[...], sin_ref[...])
        s = _dot_nt(q.astype(BF16), kk) * scale
        p = jnp.exp(s - jnp.max(s, axis=-1, keepdims=True))
        den = jnp.sum(p, axis=-1, keepdims=True)
        o = _dot(p.astype(BF16), vv) / den
        o_ref[:, cols] = o.astype(o_ref.dtype)


def _attention(qkv, kb, vb, q_norm, rows0, seq, nb, tk, rope_tabs):
    tq = 256
    gw = GQA_GROUP * HEAD_DIM
    nq = seq // tq
    rb0 = rows0 // tq
    in_specs = [pl.BlockSpec((tq, gw), lambda b, h, i: (rb0 + b * nq + i, h)),
                pl.BlockSpec((tk, HEAD_DIM), lambda b, h, i: (b, h)),
                pl.BlockSpec((tk, HEAD_DIM), lambda b, h, i: (b, h)),
                pl.BlockSpec((1, HEAD_DIM), lambda b, h, i: (0, 0))]
    args = [qkv, kb, vb, q_norm.reshape(1, HEAD_DIM)]
    if rope_tabs is not None:
        in_specs += [pl.BlockSpec((tq, HEAD_DIM), lambda b, h, i: (i, 0))] * 2
        args += list(rope_tabs)
    vm = 2 * (tq * gw * 4 + 2 * tk * HEAD_DIM * 2 + tq * gw * 2) + 4 * tq * tk * 4 + (4 << 20)
    return pl.pallas_call(
        functools.partial(_attn_kernel, rope=rope_tabs is not None),
        grid=(nb, N_KV_HEADS, nq),
        in_specs=in_specs,
        out_specs=pl.BlockSpec((tq, gw), lambda b, h, i: (b * nq + i, h)),
        out_shape=jax.ShapeDtypeStruct((nb * seq, D_MODEL), BF16),
        compiler_params=_cparams(("parallel", "parallel", "parallel"), vm),
        name="attention",
    )(*args)


def _rope_tables():
    n_rows = DEC_SEQ // GRID_W
    row = jnp.repeat(jnp.arange(n_rows), GRID_W).astype(F32)
    colp = jnp.tile(jnp.arange(GRID_W), n_rows).astype(F32)
    inv = ROPE_THETA ** (-jnp.arange(0, ROPE_AXIS_DIM, 2, dtype=F32) / ROPE_AXIS_DIM)
    ang_r = row[:, None] * inv
    ang_c = colp[:, None] * inv
    ang = jnp.concatenate([ang_r, ang_r, ang_c, ang_c], axis=-1)
    quarter = ROPE_AXIS_DIM // 2
    sign = jnp.where((jnp.arange(HEAD_DIM) % ROPE_AXIS_DIM) < quarter, -1.0, 1.0).astype(F32)
    return jnp.cos(ang), jnp.sin(ang) * sign


def _attn_mixer(hbf, w_qkv, q_norm, k_norm, w_out, cache_k, cache_v, x, gate):
    qkv = _matmul(hbf, w_qkv, 1024, 512)
    kw = N_KV_HEADS * HEAD_DIM
    tabs = _rope_tables()
    kn_ctx, kb_ctx = _k_prep(qkv, k_norm, 0, N_CTX, None)
    _, kb_lat = _k_prep(qkv, k_norm, N_CTX, N_LAT, tabs)
    v_all = qkv[:, N_HEADS * HEAD_DIM + kw:]
    v_ctx = v_all[:N_CTX]
    vb_ctx = v_ctx.astype(BF16)
    o_ctx = _attention(qkv, kb_ctx, vb_ctx, q_norm, 0, SEQ, BATCH, SEQ, None)
    tk = PAST_LEN + DEC_SEQ
    keys = jnp.concatenate([cache_k.reshape(DEC_BATCH, PAST_LEN, kw).astype(BF16),
                            kb_lat.reshape(DEC_BATCH, DEC_SEQ, kw)], axis=1).reshape(DEC_BATCH * tk, kw)
    vals = jnp.concatenate([cache_v.reshape(DEC_BATCH, PAST_LEN, kw).astype(BF16),
                            v_all[N_CTX:].astype(BF16).reshape(DEC_BATCH, DEC_SEQ, kw)],
                           axis=1).reshape(DEC_BATCH * tk, kw)
    o_lat = _attention(qkv, keys, vals, q_norm, N_CTX, DEC_SEQ, DEC_BATCH, tk, tabs)
    o = jnp.concatenate([o_ctx, o_lat], axis=0)
    new_k = kn_ctx.reshape(BATCH, SEQ, N_KV_HEADS, HEAD_DIM)
    new_v = v_ctx.reshape(BATCH, SEQ, N_KV_HEADS, HEAD_DIM)
    return _matmul_residual(o, w_out, x, gate, 1024, 512), new_k, new_v


def kernel(x_prompt, x_sample, state_hgrn, cache_k, cache_v, c, c_ctx, ada_w, ada_b, norm_w, final_norm_w,
           hgrn_w_in, hgrn_lb_logits, hgrn_norm_w, hgrn_w_out, conv_w_in, conv_w, conv_w_out,
           attn_w_qkv, attn_q_norm, attn_k_norm, attn_w_out, ffn_w_gate_up, ffn_w_down,
           moe_w_router, moe_w_gate_up, moe_w_down):
    x = jnp.concatenate([x_prompt.reshape(N_CTX, D_MODEL), x_sample.reshape(N_LAT, D_MODEL)], axis=0)
    conds = jnp.zeros((COND_PAD, D_MODEL), F32).at[0].set(c_ctx).at[1:N_COND].set(c)
    mod = _modulation(conds, ada_w, ada_b)
    mod = mod.reshape(DEPTH, COND_PAD, 6, 1, D_MODEL).transpose(0, 2, 1, 3, 4)

    probs = jax.nn.softmax(hgrn_lb_logits.astype(F32), axis=0)
    csum = jnp.cumsum(probs, axis=0)
    lower_bounds = csum - csum[:1]

    new_states, new_k, new_v = [], None, None
    for layer in range(DEPTH):
        kind = layer % N_MIXERS
        j = layer // N_MIXERS
        sh1, sc1, g1, sh2, sc2, g2 = (mod[layer, t] for t in range(6))
        hbf = _norm_mod(x, norm_w[layer, 0], sc1, sh1)
        if kind == 0:
            x, st = _hgrn_mixer(hbf, hgrn_w_in[j].astype(BF16), lower_bounds[j], hgrn_norm_w[j],
                                hgrn_w_out[j].astype(BF16), state_hgrn[:, j], x, g1)
            new_states.append(st)
        elif kind == 1:
            x = _conv_mixer(hbf, conv_w_in[j].astype(BF16), conv_w[j], conv_w_out[j].astype(BF16), x, g1)
        else:
            x, new_k, new_v = _attn_mixer(hbf, attn_w_qkv[j].astype(BF16), attn_q_norm[j], attn_k_norm[j],
                                          attn_w_out[j].astype(BF16), cache_k[:, j], cache_v[:, j], x, g1)
        f = layer // 2
        if layer % 2 == 0:
            hbf = _norm_mod(x, norm_w[layer, 1], sc2, sh2)
            x = _dense_ffn(hbf, ffn_w_gate_up[f].astype(BF16), ffn_w_down[f].astype(BF16), x, g2)
        else:
            x = _moe_ffn(x, norm_w[layer, 1], sc2, sh2, g2, moe_w_router[f],
                         moe_w_gate_up[f].astype(BF16), moe_w_down[f].astype(BF16))
    y = _final_norm(x, final_norm_w)
    y_prompt = y[:N_CTX].reshape(BATCH, SEQ, D_MODEL)
    y_sample = y[N_CTX:].reshape(DEC_BATCH, DEC_SEQ, D_MODEL)
    new_state_hgrn = jnp.stack(new_states, axis=1)
    return (y_prompt, y_sample, new_state_hgrn, new_k[:, None], new_v[:, None])
```

```python
import functools
import math

import numpy as np
import jax
import jax.numpy as jnp
from jax import lax
from jax.experimental import pallas as pl
from jax.experimental.pallas import tpu as pltpu

F32 = jnp.float32
BF16 = jnp.bfloat16

D_MODEL = 2048
BATCH = 32
SEQ = 256
DEPTH = 4
DEC_BATCH = 8
DEC_SEQ = 2048
PAST_LEN = 512
GRID_W = 64
N_MIXERS = 3
EPS = 1e-6
HG_HEADS = 16
HG_DK = 128
HG_DV = 128
HG_FDIM = HG_HEADS * HG_DK
CONV_W = 3
HEAD_DIM = 128
N_HEADS = 16
N_KV_HEADS = 4
GQA_GROUP = 4
ROPE_THETA = 10000.0
ROPE_AXIS_DIM = HEAD_DIM // 2
D_FF = 5632
N_EXPERTS = 8
TOP_K = 2

N_CTX = BATCH * SEQ
N_LAT = DEC_BATCH * DEC_SEQ
N_TOK = N_CTX + N_LAT
N_COND = 1 + DEC_BATCH
COND_PAD = 16

V7X_LANES = 128
V7X_VMEM_BYTES = 64 * 1024 * 1024
VMEM_CAP = 56 * 1024 * 1024

HG_CHUNK = 64
HG_LEVELS = int(math.log2(HG_CHUNK))
HG_TOT_ROWS = 16
HG_SPLIT = 3
HG_PAIR_W = 2 * HG_DK
HG_TB = 256


def _cparams(sem, vmem_bytes):
    return pltpu.CompilerParams(dimension_semantics=sem,
                                vmem_limit_bytes=int(min(VMEM_CAP, max(vmem_bytes, 16 * 1024 * 1024))))


def _cond_of_tile(i, tm):
    r = i * tm
    return jnp.where(r < N_CTX, 0, 1 + (r - N_CTX) // DEC_SEQ)


def _silu(x):
    return x * jax.nn.sigmoid(x)


def _dot(a, b):
    return jnp.dot(a, b, preferred_element_type=F32)


def _dot_nt(a, b):
    return lax.dot_general(a, b, (((1,), (1,)), ((), ())), preferred_element_type=F32)


def _dot_tn(a, b):
    return lax.dot_general(a, b, (((0,), (0,)), ((), ())), preferred_element_type=F32)


def _mod_kernel(c_ref, w_ref, b_ref, o_ref):
    a = _silu(c_ref[...]).astype(BF16)
    o_ref[...] = _dot(a, w_ref[...].astype(BF16)) + b_ref[...]


def _modulation(conds, ada_w, ada_b):
    tn = 1024
    n_out = 6 * D_MODEL
    return pl.pallas_call(
        _mod_kernel,
        grid=(DEPTH, n_out // tn),
        in_specs=[pl.BlockSpec((COND_PAD, D_MODEL), lambda l, j: (0, 0)),
                  pl.BlockSpec((None, D_MODEL, tn), lambda l, j: (l, 0, j)),
                  pl.BlockSpec((None, 1, tn), lambda l, j: (l, 0, j))],
        out_specs=pl.BlockSpec((None, COND_PAD, tn), lambda l, j: (l, 0, j)),
        out_shape=jax.ShapeDtypeStruct((DEPTH, COND_PAD, n_out), F32),
        compiler_params=_cparams(("parallel", "parallel"), 3 * D_MODEL * tn * 4 + (4 << 20)),
        name="modulation",
    )(conds, ada_w, ada_b.reshape(DEPTH, 1, n_out))


def _norm_body(x_ref, w_ref, sc_ref, sh_ref):
    x = x_ref[...]
    y = x * lax.rsqrt(jnp.mean(x * x, axis=-1, keepdims=True) + EPS) * w_ref[...]
    return y * (1.0 + sc_ref[...]) + sh_ref[...]


def _norm_mod_kernel(x_ref, w_ref, sc_ref, sh_ref, o_ref):
    o_ref[...] = _norm_body(x_ref, w_ref, sc_ref, sh_ref).astype(o_ref.dtype)


def _final_norm_kernel(x_ref, w_ref, o_ref):
    x = x_ref[...]
    o_ref[...] = x * lax.rsqrt(jnp.mean(x * x, axis=-1, keepdims=True) + EPS) * w_ref[...]


def _norm_router_kernel(x_ref, w_ref, sc_ref, sh_ref, rh_ref, rl_ref, h_ref, r_ref):
    h = _norm_body(x_ref, w_ref, sc_ref, sh_ref)
    h_ref[...] = h
    hh = h.astype(BF16)
    hl = (h - hh.astype(F32)).astype(BF16)
    logits = _dot(hh, rh_ref[...]) + (_dot(hh, rl_ref[...]) + _dot(hl, rh_ref[...]))
    lane = lax.broadcasted_iota(jnp.int32, logits.shape, 1)
    neg = jnp.float32(-jnp.inf)
    logits = jnp.where(lane < N_EXPERTS, logits, neg)
    v1 = jnp.max(logits, axis=-1, keepdims=True)
    i1 = jnp.min(jnp.where(logits == v1, lane, V7X_LANES), axis=-1, keepdims=True)
    rest = jnp.where(lane == i1, neg, logits)
    v2 = jnp.max(rest, axis=-1, keepdims=True)
    i2 = jnp.min(jnp.where(rest == v2, lane, V7X_LANES), axis=-1, keepdims=True)
    e = jnp.exp(v2 - v1)
    g1 = 1.0 / (1.0 + e)
    g2 = e * g1
    out = jnp.where(lane == 0, i1.astype(F32), 0.0)
    out = jnp.where(lane == 1, i2.astype(F32), out)
    out = jnp.where(lane == 2, g1, out)
    out = jnp.where(lane == 3, g2, out)
    r_ref[...] = out


_NORM_TM = 512


def _row_specs(tm):
    x_spec = pl.BlockSpec((tm, D_MODEL), lambda i: (i, 0))
    w_spec = pl.BlockSpec((1, D_MODEL), lambda i: (0, 0))
    c_spec = pl.BlockSpec((None, 1, D_MODEL), lambda i: (_cond_of_tile(i, tm), 0, 0))
    return x_spec, w_spec, c_spec


def _norm_mod(x, w, sc, sh, out_dtype=BF16):
    tm = _NORM_TM
    x_spec, w_spec, c_spec = _row_specs(tm)
    return pl.pallas_call(
        _norm_mod_kernel,
        grid=(N_TOK // tm,),
        in_specs=[x_spec, w_spec, c_spec, c_spec],
        out_specs=x_spec,
        out_shape=jax.ShapeDtypeStruct((N_TOK, D_MODEL), out_dtype),
        compiler_params=_cparams(("parallel",), 6 * tm * D_MODEL * 4),
        name="norm_mod",
    )(x, w.reshape(1, D_MODEL), sc, sh)


def _final_norm(x, w):
    tm = _NORM_TM
    x_spec, w_spec, _ = _row_specs(tm)
    return pl.pallas_call(
        _final_norm_kernel,
        grid=(N_TOK // tm,),
        in_specs=[x_spec, w_spec],
        out_specs=x_spec,
        out_shape=jax.ShapeDtypeStruct((N_TOK, D_MODEL), F32),
        compiler_params=_cparams(("parallel",), 6 * tm * D_MODEL * 4),
        name="final_norm",
    )(x, w.reshape(1, D_MODEL))


def _norm_router(x, w, sc, sh, w_router):
    tm = _NORM_TM
    x_spec, w_spec, c_spec = _row_specs(tm)
    wr = jnp.zeros((D_MODEL, V7X_LANES), F32).at[:, :N_EXPERTS].set(w_router)
    wr_hi = wr.astype(BF16)
    wr_lo = (wr - wr_hi.astype(F32)).astype(BF16)
    r_spec = pl.BlockSpec((D_MODEL, V7X_LANES), lambda i: (0, 0))
    return pl.pallas_call(
        _norm_router_kernel,
        grid=(N_TOK // tm,),
        in_specs=[x_spec, w_spec, c_spec, c_spec, r_spec, r_spec],
        out_specs=[x_spec, pl.BlockSpec((tm, V7X_LANES), lambda i: (i, 0))],
        out_shape=[jax.ShapeDtypeStruct((N_TOK, D_MODEL), F32),
                   jax.ShapeDtypeStruct((N_TOK, V7X_LANES), F32)],
        compiler_params=_cparams(("parallel",), 8 * tm * D_MODEL * 4),
        name="norm_router",
    )(x, w.reshape(1, D_MODEL), sc, sh, wr_hi, wr_lo)


def _mm_kernel(a_ref, w_ref, o_ref):
    o_ref[...] = _dot(a_ref[...], w_ref[...]).astype(o_ref.dtype)


def _mm_res_kernel(a_ref, w_ref, x_ref, g_ref, o_ref):
    o_ref[...] = x_ref[...] + g_ref[...] * _dot(a_ref[...], w_ref[...])


def _matmul(a, w, tm, tn, out_dtype=F32):
    m, k = a.shape
    n = w.shape[1]
    vm = 2 * (tm * k * 2 + k * tn * 2 + tm * tn * 4) + tm * tn * 4
    return pl.pallas_call(
        _mm_kernel,
        grid=(m // tm, n // tn),
        in_specs=[pl.BlockSpec((tm, k), lambda i, j: (i, 0)),
                  pl.BlockSpec((k, tn), lambda i, j: (0, j))],
        out_specs=pl.BlockSpec((tm, tn), lambda i, j: (i, j)),
        out_shape=jax.ShapeDtypeStruct((m, n), out_dtype),
        compiler_params=_cparams(("parallel", "parallel"), vm + (4 << 20)),
        name="matmul",
    )(a, w)


def _matmul_residual(a, w, x, gate, tm, tn):
    m, k = a.shape
    n = w.shape[1]
    vm = 2 * (tm * k * 2 + k * tn * 2 + 2 * tm * tn * 4) + tm * tn * 4
    return pl.pallas_call(
        _mm_res_kernel,
        grid=(m // tm, n // tn),
        in_specs=[pl.BlockSpec((tm, k), lambda i, j: (i, 0)),
                  pl.BlockSpec((k, tn), lambda i, j: (0, j)),
                  pl.BlockSpec((tm, tn), lambda i, j: (i, j)),
                  pl.BlockSpec((None, 1, tn), lambda i, j: (_cond_of_tile(i, tm), 0, j))],
        out_specs=pl.BlockSpec((tm, tn), lambda i, j: (i, j)),
        out_shape=jax.ShapeDtypeStruct((m, n), F32),
        compiler_params=_cparams(("parallel", "parallel"), vm + (4 << 20)),
        name="matmul_residual",
    )(a, w, x, gate)


_FFN_TF = 512


def _ffn_accumulate(nu_ref, x_ref, wa_ref, wg_ref, wd_ref, acc_ref, finish):
    i = pl.program_id(0)
    f = pl.program_id(1)

    @pl.when(i < nu_ref[0])
    def _():
        x = x_ref[...]
        a = _dot(x, wa_ref[...])
        g = _dot(x, wg_ref[...])
        hid = (_silu(a) * g).astype(BF16)
        part = _dot(hid, wd_ref[...])

        @pl.when(f == 0)
        def _():
            acc_ref[...] = part

        @pl.when(f > 0)
        def _():
            acc_ref[...] += part

        @pl.when(f == pl.num_programs(1) - 1)
        def _():
            finish(acc_ref[...])


def _ffn_moe_kernel(te_ref, nu_ref, x_ref, wa_ref, wg_ref, wd_ref, o_ref, acc_ref):
    def finish(acc):
        o_ref[...] = acc
    _ffn_accumulate(nu_ref, x_ref, wa_ref, wg_ref, wd_ref, acc_ref, finish)

    @pl.when((pl.program_id(0) >= nu_ref[0]) & (pl.program_id(1) == 0))
    def _():
        o_ref[...] = jnp.zeros(o_ref.shape, o_ref.dtype)


def _ffn_res_kernel(te_ref, nu_ref, x_ref, wa_ref, wg_ref, wd_ref, r_ref, g_ref, o_ref, acc_ref):
    def finish(acc):
        o_ref[...] = r_ref[...] + g_ref[...] * acc
    _ffn_accumulate(nu_ref, x_ref, wa_ref, wg_ref, wd_ref, acc_ref, finish)


def _ffn_call(xb, w_gu, w_down, tile_expert, n_used, tm, residual=None):
    rows = xb.shape[0]
    tf = _FFN_TF
    nf = D_FF // tf
    in_specs = [pl.BlockSpec((tm, D_MODEL), lambda i, f, te, nu: (i, 0)),
                pl.BlockSpec((None, D_MODEL, tf), lambda i, f, te, nu: (te[i], 0, f)),
                pl.BlockSpec((None, D_MODEL, tf), lambda i, f, te, nu: (te[i], 0, f + nf)),
                pl.BlockSpec((None, tf, D_MODEL), lambda i, f, te, nu: (te[i], f, 0))]
    args = [xb, w_gu, w_gu, w_down]
    if residual is None:
        body = _ffn_moe_kernel
    else:
        body = _ffn_res_kernel
        x, gate = residual
        in_specs += [pl.BlockSpec((tm, D_MODEL), lambda i, f, te, nu: (i, 0)),
                     pl.BlockSpec((None, 1, D_MODEL), lambda i, f, te, nu: (_cond_of_tile(i, tm), 0, 0))]
        args += [x, gate]
    vm = 2 * (tm * D_MODEL * 2 + 3 * D_MODEL * tf * 2 + 2 * tm * D_MODEL * 4) + tm * D_MODEL * 4 + 3 * tm * tf * 4
    return pl.pallas_call(
        body,
        grid_spec=pltpu.PrefetchScalarGridSpec(
            num_scalar_prefetch=2,
            grid=(rows // tm, nf),
            in_specs=in_specs,
            out_specs=pl.BlockSpec((tm, D_MODEL), lambda i, f, te, nu: (i, 0)),
            scratch_shapes=[pltpu.VMEM((tm, D_MODEL), F32)]),
        out_shape=jax.ShapeDtypeStruct((rows, D_MODEL), F32),
        compiler_params=_cparams(("parallel", "arbitrary"), vm + (4 << 20)),
        name="swiglu",
    )(tile_expert, n_used, *args)


def _dense_ffn(hb, w_gu, w_down, x, gate):
    tm = 512
    nt = N_TOK // tm
    return _ffn_call(hb, w_gu[None], w_down[None], jnp.zeros((nt,), jnp.int32),
                     jnp.full((1,), nt, jnp.int32), tm, residual=(x, gate))


_MOE_TM = 512
_MOE_ROWS = TOP_K * N_TOK + N_EXPERTS * _MOE_TM
_GATHER_TM = 256


def _gather_kernel(idx_ref, h_hbm, o_ref, buf, sem):
    i = pl.program_id(0)
    tm = o_ref.shape[0]

    def issue(tile, slot):
        def body(r, carry):
            src = h_hbm.at[pl.ds(idx_ref[tile * tm + r], 1), :]
            pltpu.make_async_copy(src, buf.at[slot, pl.ds(r, 1), :], sem.at[slot]).start()
            return carry
        lax.fori_loop(0, tm, body, 0)

    @pl.when(i == 0)
    def _():
        issue(0, 0)

    @pl.when(i + 1 < pl.num_programs(0))
    def _():
        issue(i + 1, (i + 1) % 2)

    slot = i % 2
    pltpu.make_async_copy(h_hbm.at[pl.ds(0, tm), :], buf.at[slot], sem.at[slot]).wait()
    o_ref[...] = buf[slot].astype(o_ref.dtype)


def _gather_rows(h, idx, rows):
    tm = _GATHER_TM
    return pl.pallas_call(
        _gather_kernel,
        grid_spec=pltpu.PrefetchScalarGridSpec(
            num_scalar_prefetch=1,
            grid=(rows // tm,),
            in_specs=[pl.BlockSpec(memory_space=pl.ANY)],
            out_specs=pl.BlockSpec((tm, D_MODEL), lambda i, idx: (i, 0)),
            scratch_shapes=[pltpu.VMEM((2, tm, D_MODEL), F32), pltpu.SemaphoreType.DMA((2,))]),
        out_shape=jax.ShapeDtypeStruct((rows, D_MODEL), BF16),
        compiler_params=_cparams(("arbitrary",), 6 * tm * D_MODEL * 4),
        name="moe_gather",
    )(idx, h)


def _combine_kernel(d0_ref, d1_ref, y_hbm, x_ref, gt_ref, w_ref, o_ref, buf, sem):
    i = pl.program_id(0)
    tm = o_ref.shape[0]

    def issue(tile, slot):
        def body(r, carry):
            t = tile * tm + r
            pltpu.make_async_copy(y_hbm.at[pl.ds(d0_ref[t], 1), :], buf.at[slot, 0, pl.ds(r, 1), :],
                                  sem.at[slot]).start()
            pltpu.make_async_copy(y_hbm.at[pl.ds(d1_ref[t], 1), :], buf.at[slot, 1, pl.ds(r, 1), :],
                                  sem.at[slot]).start()
            return carry
        lax.fori_loop(0, tm, body, 0)

    @pl.when(i == 0)
    def _():
        issue(0, 0)

    @pl.when(i + 1 < pl.num_programs(0))
    def _():
        issue(i + 1, (i + 1) % 2)

    slot = i % 2
    pltpu.make_async_copy(y_hbm.at[pl.ds(0, tm), :], buf.at[slot, 0], sem.at[slot]).wait()
    pltpu.make_async_copy(y_hbm.at[pl.ds(0, tm), :], buf.at[slot, 1], sem.at[slot]).wait()
    w = w_ref[...]
    y = w[:, 2:3] * buf[slot, 0] + w[:, 3:4] * buf[slot, 1]
    o_ref[...] = x_ref[...] + gt_ref[...] * y


def _combine_rows(y, d0, d1, x, gate, route):
    tm = _GATHER_TM
    return pl.pallas_call(
        _combine_kernel,
        grid_spec=pltpu.PrefetchScalarGridSpec(
            num_scalar_prefetch=2,
            grid=(N_TOK // tm,),
            in_specs=[pl.BlockSpec(memory_space=pl.ANY),
                      pl.BlockSpec((tm, D_MODEL), lambda i, a, b: (i, 0)),
                      pl.BlockSpec((None, 1, D_MODEL), lambda i, a, b: (_cond_of_tile(i, tm), 0, 0)),
                      pl.BlockSpec((tm, V7X_LANES), lambda i, a, b: (i, 0))],
            out_specs=pl.BlockSpec((tm, D_MODEL), lambda i, a, b: (i, 0)),
            scratch_shapes=[pltpu.VMEM((2, 2, tm, D_MODEL), F32), pltpu.SemaphoreType.DMA((2,))]),
        out_shape=jax.ShapeDtypeStruct((N_TOK, D_MODEL), F32),
        compiler_params=_cparams(("arbitrary",), 10 * tm * D_MODEL * 4),
        name="moe_combine",
    )(d0, d1, y, x, gate, route)


def _moe_ffn(x, norm_w, sc, sh, gate, w_router, w_gu, w_down):
    h, route = _norm_router(x, norm_w, sc, sh, w_router)
    top_i = route[:, :TOP_K].astype(jnp.int32)
    member = jnp.sum(top_i[:, :, None] == jnp.arange(N_EXPERTS, dtype=jnp.int32), axis=1, dtype=jnp.int32)
    before = jnp.cumsum(member, axis=0) - member
    counts = before[-1] + member[-1]
    padded = ((counts + _MOE_TM - 1) // _MOE_TM) * _MOE_TM
    ends = jnp.cumsum(padded)
    starts = ends - padded
    dest = jnp.take(starts, top_i) + jnp.take_along_axis(before, top_i, axis=1)
    tok = jnp.broadcast_to(jnp.arange(N_TOK, dtype=jnp.int32)[:, None], (N_TOK, TOP_K))
    row_tok = jnp.zeros((_MOE_ROWS,), jnp.int32).at[dest.reshape(-1)].set(tok.reshape(-1))
    n_tiles = _MOE_ROWS // _MOE_TM
    tile_start = jnp.arange(n_tiles, dtype=jnp.int32) * _MOE_TM
    tile_expert = jnp.minimum(jnp.sum(tile_start[:, None] >= ends[None, :], axis=1), N_EXPERTS - 1).astype(jnp.int32)
    n_used = (ends[-1:] // _MOE_TM).astype(jnp.int32)
    xs = _gather_rows(h, row_tok, _MOE_ROWS)
    ys = _ffn_call(xs, w_gu, w_down, tile_expert, n_used, _MOE_TM)
    return _combine_rows(ys, dest[:, 0], dest[:, 1], x, gate, route)


@functools.lru_cache(maxsize=None)
def _hgrn_tables():
    c = HG_CHUNK
    n_rows = c * (HG_LEVELS + 1) + HG_TOT_ROWS
    prefix = np.zeros((2, n_rows, c), np.float32)
    role = np.zeros((2, HG_LEVELS, c, 1), np.float32)
    pair = np.zeros((2, HG_LEVELS, c, c), np.float32)
    for d in range(2):
        for i in range(c):
            if d == 0:
                prefix[d, i, :i + 1] = 1.0
            else:
                prefix[d, i, i:] = 1.0
        for l in range(HG_LEVELS):
            m = 2 ** l
            for i in range(c):
                blk = i // (2 * m)
                upper = (i // m) % 2 == 1
                mid = blk * 2 * m + m
                row = c * (l + 1) + i
                if d == 0:
                    is_q = upper
                    lo, hi = (mid, i + 1) if upper else (i + 1, mid)
                else:
                    is_q = not upper
                    lo, hi = (mid, i) if upper else (i, mid)
                prefix[d, row, lo:hi] = 1.0
                role[d, l, i, 0] = 1.0 if is_q else 0.0
            for i in range(c):
                for j in range(c):
                    same = i // (2 * m) == j // (2 * m)
                    if same and role[d, l, i, 0] == 1.0 and role[d, l, j, 0] == 0.0:
                        pair[d, l, i, j] = 1.0
        prefix[d, c * (HG_LEVELS + 1):, :] = 1.0
    prefix = np.concatenate([prefix] * HG_SPLIT, axis=2)
    pair = np.concatenate([pair, pair], axis=3)
    return prefix, pair


def _pair_blockdiag(x2):
    zero = jnp.zeros((x2.shape[0], HG_DK), x2.dtype)
    return jnp.concatenate([jnp.concatenate([x2[:, :HG_DK], zero], axis=1),
                            jnp.concatenate([zero, x2[:, HG_DK:]], axis=1)], axis=0)


def _role_select(qh, k, level, backward):
    m = 2 ** level
    if m % 8 == 0:
        parts = []
        for r in range(0, HG_CHUNK, m):
            upper = (r // m) % 2 == 1
            parts.append((qh if upper != backward else k)[r:r + m])
        return jnp.concatenate(parts, axis=0)
    row = lax.broadcasted_iota(jnp.int32, qh.shape, 0)
    upper = (row // m) % 2 == 1
    return jnp.where(upper != backward, qh, k)


def _hgrn_dir_kernel(*refs, backward, hb, has_init, write_state):
    it = iter(refs)
    q_ref, f_ref, v_ref, la_ref, l1_ref, pm_ref, pair_ref = (next(it) for _ in range(7))
    s0_ref = next(it) if has_init else None
    if backward:
        of_ref, gt_ref, nw_ref = next(it), next(it), next(it)
    o_ref = next(it)
    st_ref = next(it) if write_state else None
    st_scr = next(it)

    c = HG_CHUNK
    n_chunk = HG_TB // c
    n_pair = hb // 2
    t = pl.program_id(2)

    @pl.when(t == 0)
    def _():
        for p in range(n_pair):
            if has_init:
                st_scr[p] = jnp.concatenate([s0_ref[2 * p].T, s0_ref[2 * p + 1].T], axis=1)
            else:
                st_scr[p] = jnp.zeros((HG_DV, HG_PAIR_W), F32)

    def pair_unit(p, ci):
        r0 = pl.multiple_of(ci * c, c)
        cols = slice(p * HG_PAIR_W, (p + 1) * HG_PAIR_W)
        qh = _silu(q_ref[pl.ds(r0, c), cols])
        z = f_ref[pl.ds(r0, c), cols]
        v = v_ref[pl.ds(r0, c), cols]
        la = la_ref[:, cols]
        b = l1_ref[:, cols] + (jnp.minimum(z, 0.0) - jnp.log1p(jnp.exp(-jnp.abs(z))))
        g = jnp.maximum(la, b) + jnp.log1p(jnp.exp(-jnp.abs(la - b)))
        k = 1.0 - jnp.exp(g)
        g1 = g.astype(BF16)
        rem = g - g1.astype(F32)
        g2 = rem.astype(BF16)
        g3 = (rem - g2.astype(F32)).astype(BF16)
        sums = _dot(pm_ref[...], jnp.concatenate([g1, g2, g3], axis=0))
        cum = sums[0:c]
        tot = sums[c * (HG_LEVELS + 1):c * (HG_LEVELS + 1) + 1]
        scores = jnp.zeros((c, HG_PAIR_W // 2), F32)
        for l in range(HG_LEVELS):
            e = jnp.exp(sums[c * (l + 1):c * (l + 2)])
            xk = (_role_select(qh, k, l, backward) * e).astype(BF16)
            scores = scores + pair_ref[l] * _dot_nt(xk, _pair_blockdiag(xk))
        vb = v.astype(BF16)
        qk = qh * k
        lane = lax.broadcasted_iota(jnp.int32, qk.shape, 1)
        self_score = jnp.where(lane < HG_DK, jnp.sum(qk[:, :HG_DK], axis=-1, keepdims=True),
                               jnp.sum(qk[:, HG_DK:], axis=-1, keepdims=True))
        o = _dot(scores.astype(BF16), _pair_blockdiag(vb)) + self_score * v
        st = st_scr[p]
        o = o + _dot_nt((qh * jnp.exp(cum)).astype(BF16), _pair_blockdiag(st.astype(BF16)))
        kt = (k * jnp.exp(tot - cum)).astype(BF16)
        v_rows = jnp.concatenate([vb[:, :HG_DV], vb[:, HG_DV:]], axis=0)
        st_scr[p] = st * jnp.exp(tot) + _dot_tn(v_rows, _pair_blockdiag(kt))
        if backward:
            o = o + of_ref[pl.ds(r0, c), cols]
            gate = _silu(gt_ref[pl.ds(r0, c), cols])
            for hh in range(2):
                hc = slice(hh * HG_DV, (hh + 1) * HG_DV)
                oh = o[:, hc]
                y = oh * lax.rsqrt(jnp.mean(oh * oh, axis=-1, keepdims=True) + EPS) * nw_ref[...]
                o_ref[pl.ds(r0, c), pl.ds(p * HG_PAIR_W + hh * HG_DV, HG_DV)] = (y * gate[:, hc]).astype(o_ref.dtype)
        else:
            o_ref[pl.ds(r0, c), cols] = o

    def scan_body(i, carry):
        ci = n_chunk - 1 - i if backward else i
        for p in range(n_pair):
            pair_unit(p, ci)
        return carry

    lax.fori_loop(0, n_chunk, scan_body, 0)

    if write_state:
        @pl.when(t == pl.num_programs(2) - 1)
        def _():
            for p in range(n_pair):
                st = st_scr[p]
                st_ref[2 * p] = st[:, :HG_DK].T
                st_ref[2 * p + 1] = st[:, HG_DK:].T


def _hgrn_scan(p, log_lb, log1m_lb, norm_w, rows0, seq, nb, hb, s0=None, write_state=False):
    w = hb * HG_DK
    ngrp = HG_HEADS // hb
    per = D_MODEL // w
    tb = HG_TB
    n_t = seq // tb
    rb0 = rows0 // tb
    prefix, pair = _hgrn_tables()
    outs = []
    o_fwd = None
    for d in range(2):
        def tblk(t, d=d):
            return n_t - 1 - t if d == 1 else t

        def col(group, d=d, tblk=tblk):
            return pl.BlockSpec((tb, w), lambda b, g, t: (rb0 + b * n_t + tblk(t), group * per + g))

        def row(d=d, tblk=tblk):
            return pl.BlockSpec((tb, w), lambda b, g, t: (b * n_t + tblk(t), g))

        in_specs = [col(0), col(1 + d), col(3),
                    pl.BlockSpec((None, 1, w), lambda b, g, t, d=d: (d, 0, g)),
                    pl.BlockSpec((None, 1, w), lambda b, g, t, d=d: (d, 0, g)),
                    pl.BlockSpec(prefix.shape[1:], lambda b, g, t: (0, 0)),
                    pl.BlockSpec(pair.shape[1:], lambda b, g, t: (0, 0, 0))]
        args = [p, p, p, log_lb, log1m_lb, jnp.asarray(prefix[d], BF16), jnp.asarray(pair[d])]
        if s0 is not None:
            in_specs.append(pl.BlockSpec((None, None, hb, HG_DK, HG_DV), lambda b, g, t, d=d: (b, d, g, 0, 0)))
            args.append(s0)
        if d == 1:
            in_specs += [row(), col(4), pl.BlockSpec((1, HG_DV), lambda b, g, t: (0, 0))]
            args += [o_fwd, p, norm_w.reshape(1, HG_DV)]
        out_specs = [row()]
        out_shape = [jax.ShapeDtypeStruct((nb * seq, D_MODEL), BF16 if d == 1 else F32)]
        if write_state:
            out_specs.append(pl.BlockSpec((None, hb, HG_DK, HG_DV), lambda b, g, t: (b, g, 0, 0)))
            out_shape.append(jax.ShapeDtypeStruct((nb, HG_HEADS, HG_DK, HG_DV), F32))
        vm = 2 * 7 * tb * w * 4 + 6 * hb * HG_DK * HG_DV * 4 + (16 << 20)
        res = pl.pallas_call(
            functools.partial(_hgrn_dir_kernel, backward=d == 1, hb=hb, has_init=s0 is not None,
                              write_state=write_state),
            grid=(nb, ngrp, n_t),
            in_specs=in_specs,
            out_specs=out_specs,
            out_shape=out_shape,
            scratch_shapes=[pltpu.VMEM((hb // 2, HG_DV, HG_PAIR_W), F32)],
            compiler_params=_cparams(("parallel", "parallel", "arbitrary"), vm),
            name="hgrn_bwd" if d == 1 else "hgrn_fwd",
        )(*args)
        if d == 0:
            o_fwd = res[0]
        outs.append(res)
    o = outs[1][0]
    if write_state:
        return o, jnp.stack([outs[0][1], outs[1][1]], axis=1)
    return (o,)


def _hgrn_mixer(hbf, w_in, lb, norm_w, w_out, s0_lat, x, gate):
    p = _matmul(hbf, w_in, 1024, 512)
    log_lb = jnp.log(lb).reshape(2, 1, HG_FDIM)
    log1m_lb = jnp.log1p(-lb).reshape(2, 1, HG_FDIM)
    o_ctx, st = _hgrn_scan(p, log_lb, log1m_lb, norm_w, 0, SEQ, BATCH, 8, write_state=True)
    (o_lat,) = _hgrn_scan(p, log_lb, log1m_lb, norm_w, N_CTX, DEC_SEQ, DEC_BATCH, 8, s0=s0_lat)
    o = jnp.concatenate([o_ctx, o_lat], axis=0)
    return _matmul_residual(o, w_out, x, gate, 1024, 512), st


def _conv_kernel(b_ref, c_ref, x_ref, w_ref, o_ref):
    u = c_ref[...] * x_ref[...]
    seq = u.shape[0]
    row = lax.broadcasted_iota(jnp.int32, u.shape, 0)
    prev = jnp.where(row == 0, 0.0, pltpu.roll(u, 1, 0))
    nxt = jnp.where(row == seq - 1, 0.0, pltpu.roll(u, seq - 1, 0))
    w = w_ref[...]
    y = prev * w[0:1] + u * w[1:2] + nxt * w[2:3]
    o_ref[...] = (b_ref[...] * y).astype(o_ref.dtype)


def _conv_gate(p, conv_w, rows0, seq, nb):
    tc = 256
    per = D_MODEL // tc
    rb0 = rows0 // seq

    def col(group):
        return pl.BlockSpec((seq, tc), lambda b, j: (rb0 + b, group * per + j))

    return pl.pallas_call(
        _conv_kernel,
        grid=(nb, per),
        in_specs=[col(0), col(1), col(2), pl.BlockSpec((CONV_W, tc), lambda b, j: (0, j))],
        out_specs=pl.BlockSpec((seq, tc), lambda b, j: (b, j)),
        out_shape=jax.ShapeDtypeStruct((nb * seq, D_MODEL), BF16),
        compiler_params=_cparams(("parallel", "parallel"), 14 * seq * tc * 4),
        name="conv_gate",
    )(p, p, p, conv_w)


def _conv_mixer(hbf, w_in, conv_w, w_out, x, gate):
    p = _matmul(hbf, w_in, 1024, 512)
    o = jnp.concatenate([_conv_gate(p, conv_w, 0, SEQ, BATCH),
                         _conv_gate(p, conv_w, N_CTX, DEC_SEQ, DEC_BATCH)], axis=0)
    return _matmul_residual(o, w_out, x, gate, 1024, 512)


def _head_norm(x, w):
    return x * lax.rsqrt(jnp.mean(x * x, axis=-1, keepdims=True) + EPS) * w


def _rope(x, cos, sin_signed):
    lane = lax.broadcasted_iota(jnp.int32, x.shape, 1)
    first = (lane % ROPE_AXIS_DIM) < (ROPE_AXIS_DIM // 2)
    rot = jnp.where(first, pltpu.roll(x, HEAD_DIM - ROPE_AXIS_DIM // 2, 1), pltpu.roll(x, ROPE_AXIS_DIM // 2, 1))
    return x * cos + rot * sin_signed


def _kprep_kernel(*refs, rope):
    if rope:
        k_ref, w_ref, cos_ref, sin_ref, kn_ref, kb_ref = refs
    else:
        k_ref, w_ref, kn_ref, kb_ref = refs
    for h in range(N_KV_HEADS):
        cols = slice(h * HEAD_DIM, (h + 1) * HEAD_DIM)
        kn = _head_norm(k_ref[:, cols], w_ref[...])
        kn_ref[:, cols] = kn
        if rope:
            kn = _rope(kn, cos_ref[...], sin_ref[...])
        kb_ref[:, cols] = kn.astype(BF16)


def _k_prep(qkv, k_norm, rows0, nrows, rope_tabs):
    tm = 512
    kw = N_KV_HEADS * HEAD_DIM
    kcol = (N_HEADS * HEAD_DIM) // kw
    rb0 = rows0 // tm
    in_specs = [pl.BlockSpec((tm, kw), lambda i: (rb0 + i, kcol)),
                pl.BlockSpec((1, HEAD_DIM), lambda i: (0, 0))]
    args = [qkv, k_norm.reshape(1, HEAD_DIM)]
    if rope_tabs is not None:
        per = DEC_SEQ // tm
        in_specs += [pl.BlockSpec((tm, HEAD_DIM), lambda i: (i % per, 0))] * 2
        args += list(rope_tabs)
    o_spec = pl.BlockSpec((tm, kw), lambda i: (i, 0))
    return pl.pallas_call(
        functools.partial(_kprep_kernel, rope=rope_tabs is not None),
        grid=(nrows // tm,),
        in_specs=in_specs,
        out_specs=[o_spec, o_spec],
        out_shape=[jax.ShapeDtypeStruct((nrows, kw), F32), jax.ShapeDtypeStruct((nrows, kw), BF16)],
        compiler_params=_cparams(("parallel",), 12 * tm * kw * 4),
        name="k_prep",
    )(*args)


def _attn_kernel(*refs, rope):
    if rope:
        q_ref, k_ref, v_ref, w_ref, cos_ref, sin_ref, o_ref = refs
    else:
        q_ref, k_ref, v_ref, w_ref, o_ref = refs
    kk = k_ref[...]
    vv = v_ref[...]
    scale = HEAD_DIM ** -0.5
    for g in range(GQA_GROUP):
        cols = slice(g * HEAD_DIM, (g + 1) * HEAD_DIM)
        q = _head_norm(q_ref[:, cols], w_ref[...])
        if rope:
            q = _rope(q, cos_ref[...], sin_ref[...])
        s = _dot_nt(q.astype(BF16), kk) * scale
        p = jnp.exp(s - jnp.max(s, axis=-1, keepdims=True))
        den = jnp.sum(p, axis=-1, keepdims=True)
        o = _dot(p.astype(BF16), vv) / den
        o_ref[:, cols] = o.astype(o_ref.dtype)


def _attention(qkv, kb, vb, q_norm, rows0, seq, nb, tk, rope_tabs):
    tq = 256
    gw = GQA_GROUP * HEAD_DIM
    nq = seq // tq
    rb0 = rows0 // tq
    in_specs = [pl.BlockSpec((tq, gw), lambda b, h, i: (rb0 + b * nq + i, h)),
                pl.BlockSpec((tk, HEAD_DIM), lambda b, h, i: (b, h)),
                pl.BlockSpec((tk, HEAD_DIM), lambda b, h, i: (b, h)),
                pl.BlockSpec((1, HEAD_DIM), lambda b, h, i: (0, 0))]
    args = [qkv, kb, vb, q_norm.reshape(1, HEAD_DIM)]
    if rope_tabs is not None:
        in_specs += [pl.BlockSpec((tq, HEAD_DIM), lambda b, h, i: (i, 0))] * 2
        args += list(rope_tabs)
    vm = 2 * (tq * gw * 4 + 2 * tk * HEAD_DIM * 2 + tq * gw * 2) + 4 * tq * tk * 4 + (4 << 20)
    return pl.pallas_call(
        functools.partial(_attn_kernel, rope=rope_tabs is not None),
        grid=(nb, N_KV_HEADS, nq),
        in_specs=in_specs,
        out_specs=pl.BlockSpec((tq, gw), lambda b, h, i: (b * nq + i, h)),
        out_shape=jax.ShapeDtypeStruct((nb * seq, D_MODEL), BF16),
        compiler_params=_cparams(("parallel", "parallel", "parallel"), vm),
        name="attention",
    )(*args)


def _rope_tables():
    n_rows = DEC_SEQ // GRID_W
    row = jnp.repeat(jnp.arange(n_rows), GRID_W).astype(F32)
    colp = jnp.tile(jnp.arange(GRID_W), n_rows).astype(F32)
    inv = ROPE_THETA ** (-jnp.arange(0, ROPE_AXIS_DIM, 2, dtype=F32) / ROPE_AXIS_DIM)
    ang_r = row[:, None] * inv
    ang_c = colp[:, None] * inv
    ang = jnp.concatenate([ang_r, ang_r, ang_c, ang_c], axis=-1)
    quarter = ROPE_AXIS_DIM // 2
    sign = jnp.where((jnp.arange(HEAD_DIM) % ROPE_AXIS_DIM) < quarter, -1.0, 1.0).astype(F32)
    return jnp.cos(ang), jnp.sin(ang) * sign


def _attn_mixer(hbf, w_qkv, q_norm, k_norm, w_out, cache_k, cache_v, x, gate):
    qkv = _matmul(hbf, w_qkv, 1024, 512)
    kw = N_KV_HEADS * HEAD_DIM
    tabs = _rope_tables()
    kn_ctx, kb_ctx = _k_prep(qkv, k_norm, 0, N_CTX, None)
    _, kb_lat = _k_prep(qkv, k_norm, N_CTX, N_LAT, tabs)
    v_all = qkv[:, N_HEADS * HEAD_DIM + kw:]
    v_ctx = v_all[:N_CTX]
    vb_ctx = v_ctx.astype(BF16)
    o_ctx = _attention(qkv, kb_ctx, vb_ctx, q_norm, 0, SEQ, BATCH, SEQ, None)
    tk = PAST_LEN + DEC_SEQ
    keys = jnp.concatenate([cache_k.reshape(DEC_BATCH, PAST_LEN, kw).astype(BF16),
                            kb_lat.reshape(DEC_BATCH, DEC_SEQ, kw)], axis=1).reshape(DEC_BATCH * tk, kw)
    vals = jnp.concatenate([cache_v.reshape(DEC_BATCH, PAST_LEN, kw).astype(BF16),
                            v_all[N_CTX:].astype(BF16).reshape(DEC_BATCH, DEC_SEQ, kw)],
                           axis=1).reshape(DEC_BATCH * tk, kw)
    o_lat = _attention(qkv, keys, vals, q_norm, N_CTX, DEC_SEQ, DEC_BATCH, tk, tabs)
    o = jnp.concatenate([o_ctx, o_lat], axis=0)
    new_k = kn_ctx.reshape(BATCH, SEQ, N_KV_HEADS, HEAD_DIM)
    new_v = v_ctx.reshape(BATCH, SEQ, N_KV_HEADS, HEAD_DIM)
    return _matmul_residual(o, w_out, x, gate, 1024, 512), new_k, new_v


def kernel(x_prompt, x_sample, state_hgrn, cache_k, cache_v, c, c_ctx, ada_w, ada_b, norm_w, final_norm_w,
           hgrn_w_in, hgrn_lb_logits, hgrn_norm_w, hgrn_w_out, conv_w_in, conv_w, conv_w_out,
           attn_w_qkv, attn_q_norm, attn_k_norm, attn_w_out, ffn_w_gate_up, ffn_w_down,
           moe_w_router, moe_w_gate_up, moe_w_down):
    x = jnp.concatenate([x_prompt.reshape(N_CTX, D_MODEL), x_sample.reshape(N_LAT, D_MODEL)], axis=0)
    conds = jnp.zeros((COND_PAD, D_MODEL), F32).at[0].set(c_ctx).at[1:N_COND].set(c)
    mod = _modulation(conds, ada_w, ada_b)
    mod = mod.reshape(DEPTH, COND_PAD, 6, 1, D_MODEL).transpose(0, 2, 1, 3, 4)

    probs = jax.nn.softmax(hgrn_lb_logits.astype(F32), axis=0)
    csum = jnp.cumsum(probs, axis=0)
    lower_bounds = csum - csum[:1]

    new_states, new_k, new_v = [], None, None
    for layer in range(DEPTH):
        kind = layer % N_MIXERS
        j = layer // N_MIXERS
        sh1, sc1, g1, sh2, sc2, g2 = (mod[layer, t] for t in range(6))
        hbf = _norm_mod(x, norm_w[layer, 0], sc1, sh1)
        if kind == 0:
            x, st = _hgrn_mixer(hbf, hgrn_w_in[j].astype(BF16), lower_bounds[j], hgrn_norm_w[j],
                                hgrn_w_out[j].astype(BF16), state_hgrn[:, j], x, g1)
            new_states.append(st)
        elif kind == 1:
            x = _conv_mixer(hbf, conv_w_in[j].astype(BF16), conv_w[j], conv_w_out[j].astype(BF16), x, g1)
        else:
            x, new_k, new_v = _attn_mixer(hbf, attn_w_qkv[j].astype(BF16), attn_q_norm[j], attn_k_norm[j],
                                          attn_w_out[j].astype(BF16), cache_k[:, j], cache_v[:, j], x, g1)
        f = layer // 2
        if layer % 2 == 0:
            hbf = _norm_mod(x, norm_w[layer, 1], sc2, sh2)
            x = _dense_ffn(hbf, ffn_w_gate_up[f].astype(BF16), ffn_w_down[f].astype(BF16), x, g2)
        else:
            x = _moe_ffn(x, norm_w[layer, 1], sc2, sh2, g2, moe_w_router[f],
                         moe_w_gate_up[f].astype(BF16), moe_w_down[f].astype(BF16))
    y = _final_norm(x, final_norm_w)
    y_prompt = y[:N_CTX].reshape(BATCH, SEQ, D_MODEL)
    y_sample = y[N_CTX:].reshape(DEC_BATCH, DEC_SEQ, D_MODEL)
    new_state_hgrn = jnp.stack(new_states, axis=1)
    return (y_prompt, y_sample, new_state_hgrn, new_k[:, None], new_v[:, None])
```

```python
import functools
import math

import numpy as np
import jax
import jax.numpy as jnp
from jax import lax
from jax.experimental import pallas as pl
from jax.experimental.pallas import tpu as pltpu

F32 = jnp.float32
BF16 = jnp.bfloat16

D_MODEL = 2048
BATCH = 32
SEQ = 256
DEPTH = 4
DEC_BATCH = 8
DEC_SEQ = 2048
PAST_LEN = 512
GRID_W = 64
N_MIXERS = 3
EPS = 1e-6
HG_HEADS = 16
HG_DK = 128
HG_DV = 128
HG_FDIM = HG_HEADS * HG_DK
CONV_W = 3
HEAD_DIM = 128
N_HEADS = 16
N_KV_HEADS = 4
GQA_GROUP = 4
ROPE_THETA = 10000.0
ROPE_AXIS_DIM = HEAD_DIM // 2
D_FF = 5632
N_EXPERTS = 8
TOP_K = 2

N_CTX = BATCH * SEQ
N_LAT = DEC_BATCH * DEC_SEQ
N_TOK = N_CTX + N_LAT
N_COND = 1 + DEC_BATCH
COND_PAD = 16

V7X_LANES = 128
V7X_VMEM_BYTES = 64 * 1024 * 1024
VMEM_CAP = 56 * 1024 * 1024

HG_CHUNK = 64
HG_LEVELS = int(math.log2(HG_CHUNK))
HG_TOT_ROWS = 16
HG_SPLIT = 3
HG_PAIR_W = 2 * HG_DK
HG_TB = 256


def _cparams(sem, vmem_bytes):
    return pltpu.CompilerParams(dimension_semantics=sem,
                                vmem_limit_bytes=int(min(VMEM_CAP, max(vmem_bytes, 16 * 1024 * 1024))))


def _cond_of_tile(i, tm):
    r = i * tm
    return jnp.where(r < N_CTX, 0, 1 + (r - N_CTX) // DEC_SEQ)


def _silu(x):
    return x * jax.nn.sigmoid(x)


def _dot(a, b):
    return jnp.dot(a, b, preferred_element_type=F32)


def _dot_nt(a, b):
    return lax.dot_general(a, b, (((1,), (1,)), ((), ())), preferred_element_type=F32)


def _dot_tn(a, b):
    return lax.dot_general(a, b, (((0,), (0,)), ((), ())), preferred_element_type=F32)


def _mod_kernel(c_ref, w_ref, b_ref, o_ref):
    a = _silu(c_ref[...]).astype(BF16)
    o_ref[...] = _dot(a, w_ref[...].astype(BF16)) + b_ref[...]


def _modulation(conds, ada_w, ada_b):
    tn = 1024
    n_out = 6 * D_MODEL
    return pl.pallas_call(
        _mod_kernel,
        grid=(DEPTH, n_out // tn),
        in_specs=[pl.BlockSpec((COND_PAD, D_MODEL), lambda l, j: (0, 0)),
                  pl.BlockSpec((None, D_MODEL, tn), lambda l, j: (l, 0, j)),
                  pl.BlockSpec((None, 1, tn), lambda l, j: (l, 0, j))],
        out_specs=pl.BlockSpec((None, COND_PAD, tn), lambda l, j: (l, 0, j)),
        out_shape=jax.ShapeDtypeStruct((DEPTH, COND_PAD, n_out), F32),
        compiler_params=_cparams(("parallel", "parallel"), 3 * D_MODEL * tn * 4 + (4 << 20)),
        name="modulation",
    )(conds, ada_w, ada_b.reshape(DEPTH, 1, n_out))


def _norm_body(x_ref, w_ref, sc_ref, sh_ref):
    x = x_ref[...]
    y = x * lax.rsqrt(jnp.mean(x * x, axis=-1, keepdims=True) + EPS) * w_ref[...]
    return y * (1.0 + sc_ref[...]) + sh_ref[...]


def _norm_mod_kernel(x_ref, w_ref, sc_ref, sh_ref, o_ref):
    o_ref[...] = _norm_body(x_ref, w_ref, sc_ref, sh_ref).astype(o_ref.dtype)


def _final_norm_kernel(x_ref, w_ref, o_ref):
    x = x_ref[...]
    o_ref[...] = x * lax.rsqrt(jnp.mean(x * x, axis=-1, keepdims=True) + EPS) * w_ref[...]


def _norm_router_kernel(x_ref, w_ref, sc_ref, sh_ref, rh_ref, rl_ref, h_ref, r_ref):
    h = _norm_body(x_ref, w_ref, sc_ref, sh_ref)
    h_ref[...] = h
    hh = h.astype(BF16)
    hl = (h - hh.astype(F32)).astype(BF16)
    logits = _dot(hh, rh_ref[...]) + (_dot(hh, rl_ref[...]) + _dot(hl, rh_ref[...]))
    lane = lax.broadcasted_iota(jnp.int32, logits.shape, 1)
    neg = jnp.float32(-jnp.inf)
    logits = jnp.where(lane < N_EXPERTS, logits, neg)
    v1 = jnp.max(logits, axis=-1, keepdims=True)
    i1 = jnp.min(jnp.where(logits == v1, lane, V7X_LANES), axis=-1, keepdims=True)
    rest = jnp.where(lane == i1, neg, logits)
    v2 = jnp.max(rest, axis=-1, keepdims=True)
    i2 = jnp.min(jnp.where(rest == v2, lane, V7X_LANES), axis=-1, keepdims=True)
    e = jnp.exp(v2 - v1)
    g1 = 1.0 / (1.0 + e)
    g2 = e * g1
    out = jnp.where(lane == 0, i1.astype(F32), 0.0)
    out = jnp.where(lane == 1, i2.astype(F32), out)
    out = jnp.where(lane == 2, g1, out)
    out = jnp.where(lane == 3, g2, out)
    r_ref[...] = out


_NORM_TM = 512


def _row_specs(tm):
    x_spec = pl.BlockSpec((tm, D_MODEL), lambda i: (i, 0))
    w_spec = pl.BlockSpec((1, D_MODEL), lambda i: (0, 0))
    c_spec = pl.BlockSpec((None, 1, D_MODEL), lambda i: (_cond_of_tile(i, tm), 0, 0))
    return x_spec, w_spec, c_spec


def _norm_mod(x, w, sc, sh, out_dtype=BF16):
    tm = _NORM_TM
    x_spec, w_spec, c_spec = _row_specs(tm)
    return pl.pallas_call(
        _norm_mod_kernel,
        grid=(N_TOK // tm,),
        in_specs=[x_spec, w_spec, c_spec, c_spec],
        out_specs=x_spec,
        out_shape=jax.ShapeDtypeStruct((N_TOK, D_MODEL), out_dtype),
        compiler_params=_cparams(("parallel",), 6 * tm * D_MODEL * 4),
        name="norm_mod",
    )(x, w.reshape(1, D_MODEL), sc, sh)


def _final_norm(x, w, rows0, nrows):
    tm = _NORM_TM
    rb0 = rows0 // tm
    return pl.pallas_call(
        _final_norm_kernel,
        grid=(nrows // tm,),
        in_specs=[pl.BlockSpec((tm, D_MODEL), lambda i: (rb0 + i, 0)),
                  pl.BlockSpec((1, D_MODEL), lambda i: (0, 0))],
        out_specs=pl.BlockSpec((tm, D_MODEL), lambda i: (i, 0)),
        out_shape=jax.ShapeDtypeStruct((nrows, D_MODEL), F32),
        compiler_params=_cparams(("parallel",), 6 * tm * D_MODEL * 4),
        name="final_norm",
    )(x, w.reshape(1, D_MODEL))


def _norm_router(x, w, sc, sh, w_router):
    tm = _NORM_TM
    x_spec, w_spec, c_spec = _row_specs(tm)
    wr = jnp.zeros((D_MODEL, V7X_LANES), F32).at[:, :N_EXPERTS].set(w_router)
    wr_hi = wr.astype(BF16)
    wr_lo = (wr - wr_hi.astype(F32)).astype(BF16)
    r_spec = pl.BlockSpec((D_MODEL, V7X_LANES), lambda i: (0, 0))
    return pl.pallas_call(
        _norm_router_kernel,
        grid=(N_TOK // tm,),
        in_specs=[x_spec, w_spec, c_spec, c_spec, r_spec, r_spec],
        out_specs=[x_spec, pl.BlockSpec((tm, V7X_LANES), lambda i: (i, 0))],
        out_shape=[jax.ShapeDtypeStruct((N_TOK, D_MODEL), F32),
                   jax.ShapeDtypeStruct((N_TOK, V7X_LANES), F32)],
        compiler_params=_cparams(("parallel",), 8 * tm * D_MODEL * 4),
        name="norm_router",
    )(x, w.reshape(1, D_MODEL), sc, sh, wr_hi, wr_lo)


def _mm_kernel(a_ref, w_ref, o_ref):
    o_ref[...] = _dot(a_ref[...], w_ref[...].astype(BF16)).astype(o_ref.dtype)


def _mm_res_kernel(a_ref, w_ref, x_ref, g_ref, o_ref):
    o_ref[...] = x_ref[...] + g_ref[...] * _dot(a_ref[...], w_ref[...].astype(BF16))


def _matmul(a, w_all, layer, tm, tn, out_dtype=F32):
    m, k = a.shape
    n = w_all.shape[2]
    wb = w_all.dtype.itemsize
    vm = 2 * (tm * k * 2 + k * tn * wb + tm * tn * 4) + tm * tn * 4 + k * tn * 2
    return pl.pallas_call(
        _mm_kernel,
        grid=(m // tm, n // tn),
        in_specs=[pl.BlockSpec((tm, k), lambda i, j: (i, 0)),
                  pl.BlockSpec((None, k, tn), lambda i, j: (layer, 0, j))],
        out_specs=pl.BlockSpec((tm, tn), lambda i, j: (i, j)),
        out_shape=jax.ShapeDtypeStruct((m, n), out_dtype),
        compiler_params=_cparams(("parallel", "parallel"), vm + (4 << 20)),
        name="matmul",
    )(a, w_all)


def _matmul_residual(a, w_all, layer, x, gate, tm, tn):
    m, k = a.shape
    n = w_all.shape[2]
    wb = w_all.dtype.itemsize
    vm = 2 * (tm * k * 2 + k * tn * wb + 2 * tm * tn * 4) + tm * tn * 4 + k * tn * 2
    return pl.pallas_call(
        _mm_res_kernel,
        grid=(m // tm, n // tn),
        in_specs=[pl.BlockSpec((tm, k), lambda i, j: (i, 0)),
                  pl.BlockSpec((None, k, tn), lambda i, j: (layer, 0, j)),
                  pl.BlockSpec((tm, tn), lambda i, j: (i, j)),
                  pl.BlockSpec((None, 1, tn), lambda i, j: (_cond_of_tile(i, tm), 0, j))],
        out_specs=pl.BlockSpec((tm, tn), lambda i, j: (i, j)),
        out_shape=jax.ShapeDtypeStruct((m, n), F32),
        compiler_params=_cparams(("parallel", "parallel"), vm + (4 << 20)),
        name="matmul_residual",
    )(a, w_all, x, gate)


_FFN_TF = 512


def _ffn_partial(x_ref, wa_ref, wg_ref, wd_ref):
    x = x_ref[...]
    a = _dot(x, wa_ref[...])
    g = _dot(x, wg_ref[...])
    hid = (_silu(a) * g).astype(BF16)
    return _dot(hid, wd_ref[...])


def _ffn_moe_kernel(te_ref, nu_ref, x_ref, wa_ref, wg_ref, wd_ref, o_ref):
    @pl.when(pl.program_id(1) == 0)
    def _():
        o_ref[...] = jnp.zeros(o_ref.shape, o_ref.dtype)

    @pl.when(pl.program_id(0) < nu_ref[0])
    def _():
        o_ref[...] += _ffn_partial(x_ref, wa_ref, wg_ref, wd_ref)


def _ffn_res_kernel(te_ref, nu_ref, x_ref, wa_ref, wg_ref, wd_ref, r_ref, g_ref, o_ref):
    @pl.when(pl.program_id(1) == 0)
    def _():
        o_ref[...] = r_ref[...]

    o_ref[...] += g_ref[...] * _ffn_partial(x_ref, wa_ref, wg_ref, wd_ref)


def _ffn_call(xb, w_gu, w_down, layer, tile_expert, n_used, tm, residual=None):
    rows = xb.shape[0]
    tf = _FFN_TF
    nf = D_FF // tf
    in_specs = [pl.BlockSpec((tm, D_MODEL), lambda i, f, te, nu: (i, 0)),
                pl.BlockSpec((None, None, D_MODEL, tf), lambda i, f, te, nu: (layer, te[i], 0, f)),
                pl.BlockSpec((None, None, D_MODEL, tf), lambda i, f, te, nu: (layer, te[i], 0, f + nf)),
                pl.BlockSpec((None, None, tf, D_MODEL), lambda i, f, te, nu: (layer, te[i], f, 0))]
    args = [xb, w_gu, w_gu, w_down]
    if residual is None:
        body = _ffn_moe_kernel
    else:
        body = _ffn_res_kernel
        x, gate = residual
        in_specs += [pl.BlockSpec((tm, D_MODEL), lambda i, f, te, nu: (i, 0)),
                     pl.BlockSpec((None, 1, D_MODEL), lambda i, f, te, nu: (_cond_of_tile(i, tm), 0, 0))]
        args += [x, gate]
    vm = 2 * (tm * D_MODEL * 2 + 3 * D_MODEL * tf * 2 + 2 * tm * D_MODEL * 4) + tm * D_MODEL * 4 + 3 * tm * tf * 4
    return pl.pallas_call(
        body,
        grid_spec=pltpu.PrefetchScalarGridSpec(
            num_scalar_prefetch=2,
            grid=(rows // tm, nf),
            in_specs=in_specs,
            out_specs=pl.BlockSpec((tm, D_MODEL), lambda i, f, te, nu: (i, 0))),
        out_shape=jax.ShapeDtypeStruct((rows, D_MODEL), F32),
        compiler_params=_cparams(("parallel", "arbitrary"), vm + (4 << 20)),
        name="swiglu",
    )(tile_expert, n_used, *args)


def _dense_ffn(hb, w_gu, w_down, layer, x, gate):
    tm = 512
    nt = N_TOK // tm
    return _ffn_call(hb, w_gu[:, None], w_down[:, None], layer, jnp.zeros((nt,), jnp.int32),
                     jnp.full((1,), nt, jnp.int32), tm, residual=(x, gate))


_MOE_TM = 512
_MOE_ROWS = TOP_K * N_TOK + N_EXPERTS * _MOE_TM
_GATHER_TM = 256
_DMA_ISSUE_UNROLL = 8


def _gather_kernel(idx_ref, h_hbm, o_ref, buf, sem):
    i = pl.program_id(0)
    tm = o_ref.shape[0]

    def issue(tile, slot):
        def body(r, carry):
            src = h_hbm.at[pl.ds(idx_ref[tile * tm + r], 1), :]
            pltpu.make_async_copy(src, buf.at[slot, pl.ds(r, 1), :], sem.at[slot]).start()
            return carry
        lax.fori_loop(0, tm, body, 0, unroll=_DMA_ISSUE_UNROLL)

    @pl.when(i == 0)
    def _():
        issue(0, 0)

    @pl.when(i + 1 < pl.num_programs(0))
    def _():
        issue(i + 1, (i + 1) % 2)

    slot = i % 2
    pltpu.make_async_copy(h_hbm.at[pl.ds(0, tm), :], buf.at[slot], sem.at[slot]).wait()
    o_ref[...] = buf[slot].astype(o_ref.dtype)


def _gather_rows(h, idx, rows):
    tm = _GATHER_TM
    return pl.pallas_call(
        _gather_kernel,
        grid_spec=pltpu.PrefetchScalarGridSpec(
            num_scalar_prefetch=1,
            grid=(rows // tm,),
            in_specs=[pl.BlockSpec(memory_space=pl.ANY)],
            out_specs=pl.BlockSpec((tm, D_MODEL), lambda i, idx: (i, 0)),
            scratch_shapes=[pltpu.VMEM((2, tm, D_MODEL), F32), pltpu.SemaphoreType.DMA((2,))]),
        out_shape=jax.ShapeDtypeStruct((rows, D_MODEL), BF16),
        compiler_params=_cparams(("arbitrary",), 6 * tm * D_MODEL * 4),
        name="moe_gather",
    )(idx, h)


def _combine_kernel(d0_ref, d1_ref, y_hbm, x_ref, gt_ref, w_ref, o_ref, buf, sem):
    i = pl.program_id(0)
    tm = o_ref.shape[0]

    def issue(tile, slot):
        def body(r, carry):
            t = tile * tm + r
            pltpu.make_async_copy(y_hbm.at[pl.ds(d0_ref[t], 1), :], buf.at[slot, 0, pl.ds(r, 1), :],
                                  sem.at[slot]).start()
            pltpu.make_async_copy(y_hbm.at[pl.ds(d1_ref[t], 1), :], buf.at[slot, 1, pl.ds(r, 1), :],
                                  sem.at[slot]).start()
            return carry
        lax.fori_loop(0, tm, body, 0, unroll=_DMA_ISSUE_UNROLL)

    @pl.when(i == 0)
    def _():
        issue(0, 0)

    @pl.when(i + 1 < pl.num_programs(0))
    def _():
        issue(i + 1, (i + 1) % 2)

    slot = i % 2
    pltpu.make_async_copy(y_hbm.at[pl.ds(0, tm), :], buf.at[slot, 0], sem.at[slot]).wait()
    pltpu.make_async_copy(y_hbm.at[pl.ds(0, tm), :], buf.at[slot, 1], sem.at[slot]).wait()
    w = w_ref[...]
    y = w[:, 2:3] * buf[slot, 0] + w[:, 3:4] * buf[slot, 1]
    o_ref[...] = x_ref[...] + gt_ref[...] * y


def _combine_rows(y, d0, d1, x, gate, route):
    tm = _GATHER_TM
    return pl.pallas_call(
        _combine_kernel,
        grid_spec=pltpu.PrefetchScalarGridSpec(
            num_scalar_prefetch=2,
            grid=(N_TOK // tm,),
            in_specs=[pl.BlockSpec(memory_space=pl.ANY),
                      pl.BlockSpec((tm, D_MODEL), lambda i, a, b: (i, 0)),
                      pl.BlockSpec((None, 1, D_MODEL), lambda i, a, b: (_cond_of_tile(i, tm), 0, 0)),
                      pl.BlockSpec((tm, V7X_LANES), lambda i, a, b: (i, 0))],
            out_specs=pl.BlockSpec((tm, D_MODEL), lambda i, a, b: (i, 0)),
            scratch_shapes=[pltpu.VMEM((2, 2, tm, D_MODEL), F32), pltpu.SemaphoreType.DMA((2,))]),
        out_shape=jax.ShapeDtypeStruct((N_TOK, D_MODEL), F32),
        compiler_params=_cparams(("arbitrary",), 10 * tm * D_MODEL * 4),
        name="moe_combine",
    )(d0, d1, y, x, gate, route)


def _moe_ffn(x, norm_w, sc, sh, gate, w_router, w_gu, w_down, layer):
    h, route = _norm_router(x, norm_w, sc, sh, w_router)
    top_i = route[:, :TOP_K].astype(jnp.int32)
    member = jnp.sum(top_i[:, :, None] == jnp.arange(N_EXPERTS, dtype=jnp.int32), axis=1, dtype=jnp.int32)
    before = jnp.cumsum(member, axis=0) - member
    counts = before[-1] + member[-1]
    padded = ((counts + _MOE_TM - 1) // _MOE_TM) * _MOE_TM
    ends = jnp.cumsum(padded)
    starts = ends - padded
    dest = jnp.take(starts, top_i) + jnp.take_along_axis(before, top_i, axis=1)
    tok = jnp.broadcast_to(jnp.arange(N_TOK, dtype=jnp.int32)[:, None], (N_TOK, TOP_K))
    row_tok = jnp.zeros((_MOE_ROWS,), jnp.int32).at[dest.reshape(-1)].set(tok.reshape(-1))
    n_tiles = _MOE_ROWS // _MOE_TM
    tile_start = jnp.arange(n_tiles, dtype=jnp.int32) * _MOE_TM
    tile_expert = jnp.minimum(jnp.sum(tile_start[:, None] >= ends[None, :], axis=1), N_EXPERTS - 1).astype(jnp.int32)
    n_used = (ends[-1:] // _MOE_TM).astype(jnp.int32)
    xs = _gather_rows(h, row_tok, _MOE_ROWS)
    ys = _ffn_call(xs, w_gu, w_down, layer, tile_expert, n_used, _MOE_TM)
    return _combine_rows(ys, dest[:, 0], dest[:, 1], x, gate, route)


@functools.lru_cache(maxsize=None)
def _hgrn_tables():
    c = HG_CHUNK
    n_rows = c * (HG_LEVELS + 1) + HG_TOT_ROWS
    prefix = np.zeros((2, n_rows, c), np.float32)
    role = np.zeros((2, HG_LEVELS, c, 1), np.float32)
    pair = np.zeros((2, HG_LEVELS, c, c), np.float32)
    for d in range(2):
        for i in range(c):
            if d == 0:
                prefix[d, i, :i + 1] = 1.0
            else:
                prefix[d, i, i:] = 1.0
        for l in range(HG_LEVELS):
            m = 2 ** l
            for i in range(c):
                blk = i // (2 * m)
                upper = (i // m) % 2 == 1
                mid = blk * 2 * m + m
                row = c * (l + 1) + i
                if d == 0:
                    is_q = upper
                    lo, hi = (mid, i + 1) if upper else (i + 1, mid)
                else:
                    is_q = not upper
                    lo, hi = (mid, i) if upper else (i, mid)
                prefix[d, row, lo:hi] = 1.0
                role[d, l, i, 0] = 1.0 if is_q else 0.0
            for i in range(c):
                for j in range(c):
                    same = i // (2 * m) == j // (2 * m)
                    if same and role[d, l, i, 0] == 1.0 and role[d, l, j, 0] == 0.0:
                        pair[d, l, i, j] = 1.0
        prefix[d, c * (HG_LEVELS + 1):, :] = 1.0
    prefix = np.concatenate([prefix] * HG_SPLIT, axis=2)
    pair = np.concatenate([pair, pair], axis=3)
    return prefix, pair


def _pair_blockdiag(x2):
    zero = jnp.zeros((x2.shape[0], HG_DK), x2.dtype)
    return jnp.concatenate([jnp.concatenate([x2[:, :HG_DK], zero], axis=1),
                            jnp.concatenate([zero, x2[:, HG_DK:]], axis=1)], axis=0)


def _role_select(qh, k, level, backward):
    m = 2 ** level
    if m % 8 == 0:
        parts = []
        for r in range(0, HG_CHUNK, m):
            upper = (r // m) % 2 == 1
            parts.append((qh if upper != backward else k)[r:r + m])
        return jnp.concatenate(parts, axis=0)
    row = lax.broadcasted_iota(jnp.int32, qh.shape, 0)
    upper = (row // m) % 2 == 1
    return jnp.where(upper != backward, qh, k)


def _hgrn_dir_kernel(*refs, backward, hb, has_init, write_state):
    it = iter(refs)
    q_ref, f_ref, v_ref, la_ref, l1_ref, pm_ref, pair_ref = (next(it) for _ in range(7))
    s0_ref = next(it) if has_init else None
    if backward:
        of_ref, gt_ref, nw_ref = next(it), next(it), next(it)
    o_ref = next(it)
    st_ref = next(it) if write_state else None
    st_scr = next(it)

    c = HG_CHUNK
    n_chunk = HG_TB // c
    n_pair = hb // 2
    t = pl.program_id(2)

    @pl.when(t == 0)
    def _():
        for p in range(n_pair):
            if has_init:
                st_scr[p] = jnp.concatenate([s0_ref[2 * p].T, s0_ref[2 * p + 1].T], axis=1)
            else:
                st_scr[p] = jnp.zeros((HG_DV, HG_PAIR_W), F32)

    def pair_unit(p, ci):
        r0 = ci * c
        cols = slice(p * HG_PAIR_W, (p + 1) * HG_PAIR_W)
        qh = _silu(q_ref[pl.ds(r0, c), cols])
        z = f_ref[pl.ds(r0, c), cols]
        v = v_ref[pl.ds(r0, c), cols]
        la = la_ref[:, cols]
        b = l1_ref[:, cols] + (jnp.minimum(z, 0.0) - jnp.log1p(jnp.exp(-jnp.abs(z))))
        g = jnp.maximum(la, b) + jnp.log1p(jnp.exp(-jnp.abs(la - b)))
        k = 1.0 - jnp.exp(g)
        g1 = g.astype(BF16)
        rem = g - g1.astype(F32)
        g2 = rem.astype(BF16)
        g3 = (rem - g2.astype(F32)).astype(BF16)
        sums = _dot(pm_ref[...], jnp.concatenate([g1, g2, g3], axis=0))
        cum = sums[0:c]
        tot = sums[c * (HG_LEVELS + 1):c * (HG_LEVELS + 1) + 1]
        scores = jnp.zeros((c, HG_PAIR_W // 2), F32)
        for l in range(HG_LEVELS):
            e = jnp.exp(sums[c * (l + 1):c * (l + 2)])
            xk = (_role_select(qh, k, l, backward) * e).astype(BF16)
            scores = scores + pair_ref[l] * _dot_nt(xk, _pair_blockdiag(xk))
        vb = v.astype(BF16)
        qk = qh * k
        lane = lax.broadcasted_iota(jnp.int32, qk.shape, 1)
        self_score = jnp.where(lane < HG_DK, jnp.sum(qk[:, :HG_DK], axis=-1, keepdims=True),
                               jnp.sum(qk[:, HG_DK:], axis=-1, keepdims=True))
        o = _dot(scores.astype(BF16), _pair_blockdiag(vb)) + self_score * v
        st = st_scr[p]
        o = o + _dot_nt((qh * jnp.exp(cum)).astype(BF16), _pair_blockdiag(st.astype(BF16)))
        kt = (k * jnp.exp(tot - cum)).astype(BF16)
        v_rows = jnp.concatenate([vb[:, :HG_DV], vb[:, HG_DV:]], axis=0)
        st_scr[p] = st * jnp.exp(tot) + _dot_tn(v_rows, _pair_blockdiag(kt))
        if backward:
            o = o + of_ref[pl.ds(r0, c), cols]
            gate = _silu(gt_ref[pl.ds(r0, c), cols])
            for hh in range(2):
                hc = slice(hh * HG_DV, (hh + 1) * HG_DV)
                oh = o[:, hc]
                y = oh * lax.rsqrt(jnp.mean(oh * oh, axis=-1, keepdims=True) + EPS) * nw_ref[...]
                o_ref[pl.ds(r0, c), pl.ds(p * HG_PAIR_W + hh * HG_DV, HG_DV)] = (y * gate[:, hc]).astype(o_ref.dtype)
        else:
            o_ref[pl.ds(r0, c), cols] = o

    for i in range(n_chunk):
        ci = n_chunk - 1 - i if backward else i
        for p in range(n_pair):
            pair_unit(p, ci)

    if write_state:
        @pl.when(t == pl.num_programs(2) - 1)
        def _():
            for p in range(n_pair):
                st = st_scr[p]
                st_ref[2 * p] = st[:, :HG_DK].T
                st_ref[2 * p + 1] = st[:, HG_DK:].T


def _hgrn_scan(p, log_lb, log1m_lb, norm_w, rows0, seq, nb, hb, s0=None, write_state=False):
    w = hb * HG_DK
    ngrp = HG_HEADS // hb
    per = D_MODEL // w
    tb = HG_TB
    n_t = seq // tb
    rb0 = rows0 // tb
    prefix, pair = _hgrn_tables()
    outs = []
    o_fwd = None
    for d in range(2):
        def tblk(t, d=d):
            return n_t - 1 - t if d == 1 else t

        def col(group, d=d, tblk=tblk):
            return pl.BlockSpec((tb, w), lambda b, g, t: (rb0 + b * n_t + tblk(t), group * per + g))

        def row(d=d, tblk=tblk):
            return pl.BlockSpec((tb, w), lambda b, g, t: (b * n_t + tblk(t), g))

        in_specs = [col(0), col(1 + d), col(3),
                    pl.BlockSpec((None, 1, w), lambda b, g, t, d=d: (d, 0, g)),
                    pl.BlockSpec((None, 1, w), lambda b, g, t, d=d: (d, 0, g)),
                    pl.BlockSpec(prefix.shape[1:], lambda b, g, t: (0, 0)),
                    pl.BlockSpec(pair.shape[1:], lambda b, g, t: (0, 0, 0))]
        args = [p, p, p, log_lb, log1m_lb, jnp.asarray(prefix[d], BF16), jnp.asarray(pair[d])]
        if s0 is not None:
            in_specs.append(pl.BlockSpec((None, None, hb, HG_DK, HG_DV), lambda b, g, t, d=d: (b, d, g, 0, 0)))
            args.append(s0)
        if d == 1:
            in_specs += [row(), col(4), pl.BlockSpec((1, HG_DV), lambda b, g, t: (0, 0))]
            args += [o_fwd, p, norm_w.reshape(1, HG_DV)]
        out_specs = [row()]
        out_shape = [jax.ShapeDtypeStruct((nb * seq, D_MODEL), BF16 if d == 1 else F32)]
        if write_state:
            out_specs.append(pl.BlockSpec((None, hb, HG_DK, HG_DV), lambda b, g, t: (b, g, 0, 0)))
            out_shape.append(jax.ShapeDtypeStruct((nb, HG_HEADS, HG_DK, HG_DV), F32))
        vm = 2 * 7 * tb * w * 4 + 6 * hb * HG_DK * HG_DV * 4 + (16 << 20)
        res = pl.pallas_call(
            functools.partial(_hgrn_dir_kernel, backward=d == 1, hb=hb, has_init=s0 is not None,
                              write_state=write_state),
            grid=(nb, ngrp, n_t),
            in_specs=in_specs,
            out_specs=out_specs,
            out_shape=out_shape,
            scratch_shapes=[pltpu.VMEM((hb // 2, HG_DV, HG_PAIR_W), F32)],
            compiler_params=_cparams(("parallel", "parallel", "arbitrary"), vm),
            name="hgrn_bwd" if d == 1 else "hgrn_fwd",
        )(*args)
        if d == 0:
            o_fwd = res[0]
        outs.append(res)
    o = outs[1][0]
    if write_state:
        return o, jnp.stack([outs[0][1], outs[1][1]], axis=1)
    return (o,)


def _hgrn_mixer(hbf, w_in, w_out, layer, lb, norm_w, s0_lat, x, gate):
    p = _matmul(hbf, w_in, layer, 1024, 512)
    log_lb = jnp.log(lb).reshape(2, 1, HG_FDIM)
    log1m_lb = jnp.log1p(-lb).reshape(2, 1, HG_FDIM)
    o_ctx, st = _hgrn_scan(p, log_lb, log1m_lb, norm_w, 0, SEQ, BATCH, 8, write_state=True)
    (o_lat,) = _hgrn_scan(p, log_lb, log1m_lb, norm_w, N_CTX, DEC_SEQ, DEC_BATCH, 8, s0=s0_lat)
    o = jnp.concatenate([o_ctx, o_lat], axis=0)
    return _matmul_residual(o, w_out, layer, x, gate, 1024, 512), st


def _conv_kernel(b_ref, c_ref, x_ref, w_ref, o_ref):
    u = c_ref[...] * x_ref[...]
    seq = u.shape[0]
    row = lax.broadcasted_iota(jnp.int32, u.shape, 0)
    prev = jnp.where(row == 0, 0.0, pltpu.roll(u, 1, 0))
    nxt = jnp.where(row == seq - 1, 0.0, pltpu.roll(u, seq - 1, 0))
    w = w_ref[...]
    y = prev * w[0:1] + u * w[1:2] + nxt * w[2:3]
    o_ref[...] = (b_ref[...] * y).astype(o_ref.dtype)


def _conv_gate(p, conv_w, rows0, seq, nb):
    tc = 256
    per = D_MODEL // tc
    rb0 = rows0 // seq

    def col(group):
        return pl.BlockSpec((seq, tc), lambda b, j: (rb0 + b, group * per + j))

    return pl.pallas_call(
        _conv_kernel,
        grid=(nb, per),
        in_specs=[col(0), col(1), col(2), pl.BlockSpec((CONV_W, tc), lambda b, j: (0, j))],
        out_specs=pl.BlockSpec((seq, tc), lambda b, j: (b, j)),
        out_shape=jax.ShapeDtypeStruct((nb * seq, D_MODEL), BF16),
        compiler_params=_cparams(("parallel", "parallel"), 14 * seq * tc * 4),
        name="conv_gate",
    )(p, p, p, conv_w)


def _conv_mixer(hbf, w_in, w_out, layer, conv_w, x, gate):
    p = _matmul(hbf, w_in, layer, 1024, 512)
    o = jnp.concatenate([_conv_gate(p, conv_w, 0, SEQ, BATCH),
                         _conv_gate(p, conv_w, N_CTX, DEC_SEQ, DEC_BATCH)], axis=0)
    return _matmul_residual(o, w_out, layer, x, gate, 1024, 512)


def _head_norm(x, w):
    return x * lax.rsqrt(jnp.mean(x * x, axis=-1, keepdims=True) + EPS) * w


def _rope(x, cos, sin_signed):
    lane = lax.broadcasted_iota(jnp.int32, x.shape, 1)
    first = (lane % ROPE_AXIS_DIM) < (ROPE_AXIS_DIM // 2)
    rot = jnp.where(first, pltpu.roll(x, HEAD_DIM - ROPE_AXIS_DIM // 2, 1), pltpu.roll(x, ROPE_AXIS_DIM // 2, 1))
    return x * cos + rot * sin_signed


def _kvprep_kernel(*refs, rope):
    if rope:
        k_ref, v_ref, w_ref, cos_ref, sin_ref, kb_ref, vb_ref = refs
    else:
        k_ref, v_ref, w_ref, kn_ref, vf_ref, kb_ref, vb_ref = refs
        vf_ref[...] = v_ref[...]
    vb_ref[...] = v_ref[...].astype(BF16)
    for h in range(N_KV_HEADS):
        cols = slice(h * HEAD_DIM, (h + 1) * HEAD_DIM)
        kn = _head_norm(k_ref[:, cols], w_ref[...])
        if rope:
            kn = _rope(kn, cos_ref[...], sin_ref[...])
        else:
            kn_ref[:, cols] = kn
        kb_ref[:, cols] = kn.astype(BF16)


def _kv_prep(qkv, k_norm, rows0, nrows, rope_tabs):
    tm = 512
    kw = N_KV_HEADS * HEAD_DIM
    kcol = (N_HEADS * HEAD_DIM) // kw
    rb0 = rows0 // tm
    in_specs = [pl.BlockSpec((tm, kw), lambda i: (rb0 + i, kcol)),
                pl.BlockSpec((tm, kw), lambda i: (rb0 + i, kcol + 1)),
                pl.BlockSpec((1, HEAD_DIM), lambda i: (0, 0))]
    args = [qkv, qkv, k_norm.reshape(1, HEAD_DIM)]
    o_spec = pl.BlockSpec((tm, kw), lambda i: (i, 0))
    out_specs = [o_spec, o_spec]
    out_shape = [jax.ShapeDtypeStruct((nrows, kw), BF16)] * 2
    if rope_tabs is not None:
        per = DEC_SEQ // tm
        in_specs += [pl.BlockSpec((tm, HEAD_DIM), lambda i: (i % per, 0))] * 2
        args += list(rope_tabs)
    else:
        out_specs = [o_spec, o_spec] + out_specs
        out_shape = [jax.ShapeDtypeStruct((nrows, kw), F32)] * 2 + out_shape
    return pl.pallas_call(
        functools.partial(_kvprep_kernel, rope=rope_tabs is not None),
        grid=(nrows // tm,),
        in_specs=in_specs,
        out_specs=out_specs,
        out_shape=out_shape,
        compiler_params=_cparams(("parallel",), 16 * tm * kw * 4),
        name="kv_prep",
    )(*args)


def _attn_kernel(*refs, rope):
    if rope:
        q_ref, k_ref, v_ref, w_ref, cos_ref, sin_ref, o_ref = refs
    else:
        q_ref, k_ref, v_ref, w_ref, o_ref = refs
    kk = k_ref[...]
    vv = v_ref[...]
    scale = HEAD_DIM ** -0.5
    for g in range(GQA_GROUP):
        cols = slice(g * HEAD_DIM, (g + 1) * HEAD_DIM)
        q = _head_norm(q_ref[:, cols], w_ref[...])
        if rope:
            q = _rope(q, cos_ref[...], sin_ref[...])
        s = _dot_nt(q.astype(BF16), kk) * scale
        p = jnp.exp(s - jnp.max(s, axis=-1, keepdims=True))
        den = jnp.sum(p, axis=-1, keepdims=True)
        o = _dot(p.astype(BF16), vv) / den
        o_ref[:, cols] = o.astype(o_ref.dtype)


def _attention(qkv, kb, vb, q_norm, rows0, seq, nb, tk, rope_tabs):
    tq = 256
    gw = GQA_GROUP * HEAD_DIM
    nq = seq // tq
    rb0 = rows0 // tq
    in_specs = [pl.BlockSpec((tq, gw), lambda b, h, i: (rb0 + b * nq + i, h)),
                pl.BlockSpec((tk, HEAD_DIM), lambda b, h, i: (b, h)),
                pl.BlockSpec((tk, HEAD_DIM), lambda b, h, i: (b, h)),
                pl.BlockSpec((1, HEAD_DIM), lambda b, h, i: (0, 0))]
    args = [qkv, kb, vb, q_norm.reshape(1, HEAD_DIM)]
    if rope_tabs is not None:
        in_specs += [pl.BlockSpec((tq, HEAD_DIM), lambda b, h, i: (i, 0))] * 2
        args += list(rope_tabs)
    vm = 2 * (tq * gw * 4 + 2 * tk * HEAD_DIM * 2 + tq * gw * 2) + 4 * tq * tk * 4 + (4 << 20)
    return pl.pallas_call(
        functools.partial(_attn_kernel, rope=rope_tabs is not None),
        grid=(nb, N_KV_HEADS, nq),
        in_specs=in_specs,
        out_specs=pl.BlockSpec((tq, gw), lambda b, h, i: (b * nq + i, h)),
        out_shape=jax.ShapeDtypeStruct((nb * seq, D_MODEL), BF16),
        compiler_params=_cparams(("parallel", "parallel", "parallel"), vm),
        name="attention",
    )(*args)


def _rope_tables():
    n_rows = DEC_SEQ // GRID_W
    row = jnp.repeat(jnp.arange(n_rows), GRID_W).astype(F32)
    colp = jnp.tile(jnp.arange(GRID_W), n_rows).astype(F32)
    inv = ROPE_THETA ** (-jnp.arange(0, ROPE_AXIS_DIM, 2, dtype=F32) / ROPE_AXIS_DIM)
    ang_r = row[:, None] * inv
    ang_c = colp[:, None] * inv
    ang = jnp.concatenate([ang_r, ang_r, ang_c, ang_c], axis=-1)
    quarter = ROPE_AXIS_DIM // 2
    sign = jnp.where((jnp.arange(HEAD_DIM) % ROPE_AXIS_DIM) < quarter, -1.0, 1.0).astype(F32)
    return jnp.cos(ang), jnp.sin(ang) * sign


def _attn_mixer(hbf, w_qkv, w_out, layer, q_norm, k_norm, cache_k, cache_v, x, gate):
    qkv = _matmul(hbf, w_qkv, layer, 1024, 512)
    kw = N_KV_HEADS * HEAD_DIM
    tabs = _rope_tables()
    kn_ctx, v_ctx, kb_ctx, vb_ctx = _kv_prep(qkv, k_norm, 0, N_CTX, None)
    kb_lat, vb_lat = _kv_prep(qkv, k_norm, N_CTX, N_LAT, tabs)
    o_ctx = _attention(qkv, kb_ctx, vb_ctx, q_norm, 0, SEQ, BATCH, SEQ, None)
    tk = PAST_LEN + DEC_SEQ
    keys = jnp.concatenate([cache_k.reshape(DEC_BATCH, PAST_LEN, kw).astype(BF16),
                            kb_lat.reshape(DEC_BATCH, DEC_SEQ, kw)], axis=1).reshape(DEC_BATCH * tk, kw)
    vals = jnp.concatenate([cache_v.reshape(DEC_BATCH, PAST_LEN, kw).astype(BF16),
                            vb_lat.reshape(DEC_BATCH, DEC_SEQ, kw)], axis=1).reshape(DEC_BATCH * tk, kw)
    o_lat = _attention(qkv, keys, vals, q_norm, N_CTX, DEC_SEQ, DEC_BATCH, tk, tabs)
    o = jnp.concatenate([o_ctx, o_lat], axis=0)
    new_k = kn_ctx.reshape(BATCH, SEQ, N_KV_HEADS, HEAD_DIM)
    new_v = v_ctx.reshape(BATCH, SEQ, N_KV_HEADS, HEAD_DIM)
    return _matmul_residual(o, w_out, layer, x, gate, 1024, 512), new_k, new_v


def kernel(x_prompt, x_sample, state_hgrn, cache_k, cache_v, c, c_ctx, ada_w, ada_b, norm_w, final_norm_w,
           hgrn_w_in, hgrn_lb_logits, hgrn_norm_w, hgrn_w_out, conv_w_in, conv_w, conv_w_out,
           attn_w_qkv, attn_q_norm, attn_k_norm, attn_w_out, ffn_w_gate_up, ffn_w_down,
           moe_w_router, moe_w_gate_up, moe_w_down):
    x = jnp.concatenate([x_prompt.reshape(N_CTX, D_MODEL), x_sample.reshape(N_LAT, D_MODEL)], axis=0)
    conds = jnp.zeros((COND_PAD, D_MODEL), F32).at[0].set(c_ctx).at[1:N_COND].set(c)
    mod = _modulation(conds, ada_w, ada_b)
    mod = mod.reshape(DEPTH, COND_PAD, 6, 1, D_MODEL).transpose(0, 2, 1, 3, 4)

    probs = jax.nn.softmax(hgrn_lb_logits.astype(F32), axis=0)
    csum = jnp.cumsum(probs, axis=0)
    lower_bounds = csum - csum[:1]

    ffn_gu, ffn_down = ffn_w_gate_up.astype(BF16), ffn_w_down.astype(BF16)
    moe_gu, moe_down = moe_w_gate_up.astype(BF16), moe_w_down.astype(BF16)

    new_states, new_k, new_v = [], None, None
    for layer in range(DEPTH):
        kind = layer % N_MIXERS
        j = layer // N_MIXERS
        sh1, sc1, g1, sh2, sc2, g2 = (mod[layer, t] for t in range(6))
        hbf = _norm_mod(x, norm_w[layer, 0], sc1, sh1)
        if kind == 0:
            x, st = _hgrn_mixer(hbf, hgrn_w_in, hgrn_w_out, j, lower_bounds[j], hgrn_norm_w[j],
                                state_hgrn[:, j], x, g1)
            new_states.append(st)
        elif kind == 1:
            x = _conv_mixer(hbf, conv_w_in, conv_w_out, j, conv_w[j], x, g1)
        else:
            x, new_k, new_v = _attn_mixer(hbf, attn_w_qkv, attn_w_out, j, attn_q_norm[j], attn_k_norm[j],
                                          cache_k[:, j], cache_v[:, j], x, g1)
        f = layer // 2
        if layer % 2 == 0:
            hbf = _norm_mod(x, norm_w[layer, 1], sc2, sh2)
            x = _dense_ffn(hbf, ffn_gu, ffn_down, f, x, g2)
        else:
            x = _moe_ffn(x, norm_w[layer, 1], sc2, sh2, g2, moe_w_router[f], moe_gu, moe_down, f)
    y_prompt = _final_norm(x, final_norm_w, 0, N_CTX).reshape(BATCH, SEQ, D_MODEL)
    y_sample = _final_norm(x, final_norm_w, N_CTX, N_LAT).reshape(DEC_BATCH, DEC_SEQ, D_MODEL)
    new_state_hgrn = jnp.stack(new_states, axis=1)
    return (y_prompt, y_sample, new_state_hgrn, new_k[:, None], new_v[:, None])
```

```python
import functools
import math

import numpy as np
import jax
import jax.numpy as jnp
from jax import lax
from jax.experimental import pallas as pl
from jax.experimental.pallas import tpu as pltpu

F32 = jnp.float32
BF16 = jnp.bfloat16

D_MODEL = 2048
BATCH = 32
SEQ = 256
DEPTH = 4
DEC_BATCH = 8
DEC_SEQ = 2048
PAST_LEN = 512
GRID_W = 64
N_MIXERS = 3
EPS = 1e-6
HG_HEADS = 16
HG_DK = 128
HG_DV = 128
HG_FDIM = HG_HEADS * HG_DK
CONV_W = 3
HEAD_DIM = 128
N_HEADS = 16
N_KV_HEADS = 4
GQA_GROUP = 4
ROPE_THETA = 10000.0
ROPE_AXIS_DIM = HEAD_DIM // 2
D_FF = 5632
N_EXPERTS = 8
TOP_K = 2

N_CTX = BATCH * SEQ
N_LAT = DEC_BATCH * DEC_SEQ
N_TOK = N_CTX + N_LAT
N_COND = 1 + DEC_BATCH
COND_PAD = 16

V7X_LANES = 128
V7X_VMEM_BYTES = 64 * 1024 * 1024
VMEM_CAP = 56 * 1024 * 1024

HG_CHUNK = 64
HG_LEVELS = int(math.log2(HG_CHUNK))
HG_TOT_ROWS = 16
HG_SPLIT = 2
HG_PAIR_W = 2 * HG_DK
HG_TB = 256


def _cparams(sem, vmem_bytes):
    return pltpu.CompilerParams(dimension_semantics=sem,
                                vmem_limit_bytes=int(min(VMEM_CAP, max(vmem_bytes, 16 * 1024 * 1024))))


def _cond_of_tile(i, tm):
    r = i * tm
    return jnp.where(r < N_CTX, 0, 1 + (r - N_CTX) // DEC_SEQ)


def _silu(x):
    return x * jax.nn.sigmoid(x)


def _dot(a, b):
    return jnp.dot(a, b, preferred_element_type=F32)


def _dot_nt(a, b):
    return lax.dot_general(a, b, (((1,), (1,)), ((), ())), preferred_element_type=F32)


def _dot_tn(a, b):
    return lax.dot_general(a, b, (((0,), (0,)), ((), ())), preferred_element_type=F32)


def _mod_kernel(c_ref, w_ref, b_ref, o_ref):
    a = _silu(c_ref[...]).astype(BF16)
    o_ref[...] = _dot(a, w_ref[...].astype(BF16)) + b_ref[...]


def _modulation(conds, ada_w, ada_b):
    tn = 1024
    n_out = 6 * D_MODEL
    return pl.pallas_call(
        _mod_kernel,
        grid=(DEPTH, n_out // tn),
        in_specs=[pl.BlockSpec((COND_PAD, D_MODEL), lambda l, j: (0, 0)),
                  pl.BlockSpec((None, D_MODEL, tn), lambda l, j: (l, 0, j)),
                  pl.BlockSpec((None, 1, tn), lambda l, j: (l, 0, j))],
        out_specs=pl.BlockSpec((None, COND_PAD, tn), lambda l, j: (l, 0, j)),
        out_shape=jax.ShapeDtypeStruct((DEPTH, COND_PAD, n_out), F32),
        compiler_params=_cparams(("parallel", "parallel"), 3 * D_MODEL * tn * 4 + (4 << 20)),
        name="modulation",
    )(conds, ada_w, ada_b.reshape(DEPTH, 1, n_out))


def _norm_body(x_ref, w_ref, sc_ref, sh_ref):
    x = x_ref[...]
    y = x * lax.rsqrt(jnp.mean(x * x, axis=-1, keepdims=True) + EPS) * w_ref[...]
    return y * (1.0 + sc_ref[...]) + sh_ref[...]


def _norm_mod_kernel(x_ref, w_ref, sc_ref, sh_ref, o_ref):
    o_ref[...] = _norm_body(x_ref, w_ref, sc_ref, sh_ref).astype(o_ref.dtype)


def _final_norm_kernel(x_ref, w_ref, o_ref):
    x = x_ref[...]
    o_ref[...] = x * lax.rsqrt(jnp.mean(x * x, axis=-1, keepdims=True) + EPS) * w_ref[...]


def _norm_router_kernel(x_ref, w_ref, sc_ref, sh_ref, rh_ref, rl_ref, h_ref, r_ref):
    h = _norm_body(x_ref, w_ref, sc_ref, sh_ref)
    h_ref[...] = h
    hh = h.astype(BF16)
    hl = (h - hh.astype(F32)).astype(BF16)
    logits = _dot(hh, rh_ref[...]) + (_dot(hh, rl_ref[...]) + _dot(hl, rh_ref[...]))
    lane = lax.broadcasted_iota(jnp.int32, logits.shape, 1)
    neg = jnp.float32(-jnp.inf)
    logits = jnp.where(lane < N_EXPERTS, logits, neg)
    v1 = jnp.max(logits, axis=-1, keepdims=True)
    i1 = jnp.min(jnp.where(logits == v1, lane, V7X_LANES), axis=-1, keepdims=True)
    rest = jnp.where(lane == i1, neg, logits)
    v2 = jnp.max(rest, axis=-1, keepdims=True)
    i2 = jnp.min(jnp.where(rest == v2, lane, V7X_LANES), axis=-1, keepdims=True)
    e = jnp.exp(v2 - v1)
    g1 = 1.0 / (1.0 + e)
    g2 = e * g1
    out = jnp.where(lane == 0, i1.astype(F32), 0.0)
    out = jnp.where(lane == 1, i2.astype(F32), out)
    out = jnp.where(lane == 2, g1, out)
    out = jnp.where(lane == 3, g2, out)
    r_ref[...] = out


_NORM_TM = 512


def _row_specs(tm):
    x_spec = pl.BlockSpec((tm, D_MODEL), lambda i: (i, 0))
    w_spec = pl.BlockSpec((1, D_MODEL), lambda i: (0, 0))
    c_spec = pl.BlockSpec((None, 1, D_MODEL), lambda i: (_cond_of_tile(i, tm), 0, 0))
    return x_spec, w_spec, c_spec


def _norm_mod(x, w, sc, sh, out_dtype=BF16):
    tm = _NORM_TM
    x_spec, w_spec, c_spec = _row_specs(tm)
    return pl.pallas_call(
        _norm_mod_kernel,
        grid=(N_TOK // tm,),
        in_specs=[x_spec, w_spec, c_spec, c_spec],
        out_specs=x_spec,
        out_shape=jax.ShapeDtypeStruct((N_TOK, D_MODEL), out_dtype),
        compiler_params=_cparams(("parallel",), 6 * tm * D_MODEL * 4),
        name="norm_mod",
    )(x, w.reshape(1, D_MODEL), sc, sh)


def _final_norm(x, w, rows0, nrows):
    tm = _NORM_TM
    rb0 = rows0 // tm
    return pl.pallas_call(
        _final_norm_kernel,
        grid=(nrows // tm,),
        in_specs=[pl.BlockSpec((tm, D_MODEL), lambda i: (rb0 + i, 0)),
                  pl.BlockSpec((1, D_MODEL), lambda i: (0, 0))],
        out_specs=pl.BlockSpec((tm, D_MODEL), lambda i: (i, 0)),
        out_shape=jax.ShapeDtypeStruct((nrows, D_MODEL), F32),
        compiler_params=_cparams(("parallel",), 6 * tm * D_MODEL * 4),
        name="final_norm",
    )(x, w.reshape(1, D_MODEL))


def _norm_router(x, w, sc, sh, w_router):
    tm = _NORM_TM
    x_spec, w_spec, c_spec = _row_specs(tm)
    wr = jnp.zeros((D_MODEL, V7X_LANES), F32).at[:, :N_EXPERTS].set(w_router)
    wr_hi = wr.astype(BF16)
    wr_lo = (wr - wr_hi.astype(F32)).astype(BF16)
    r_spec = pl.BlockSpec((D_MODEL, V7X_LANES), lambda i: (0, 0))
    return pl.pallas_call(
        _norm_router_kernel,
        grid=(N_TOK // tm,),
        in_specs=[x_spec, w_spec, c_spec, c_spec, r_spec, r_spec],
        out_specs=[x_spec, pl.BlockSpec((tm, V7X_LANES), lambda i: (i, 0))],
        out_shape=[jax.ShapeDtypeStruct((N_TOK, D_MODEL), F32),
                   jax.ShapeDtypeStruct((N_TOK, V7X_LANES), F32)],
        compiler_params=_cparams(("parallel",), 8 * tm * D_MODEL * 4),
        name="norm_router",
    )(x, w.reshape(1, D_MODEL), sc, sh, wr_hi, wr_lo)


_MM_TM = 1024
_MM_TN = 1024


def _mm_kernel(a_ref, w_ref, o_ref):
    o_ref[...] = _dot(a_ref[...], w_ref[...]).astype(o_ref.dtype)


def _mm_res_kernel(a_ref, w_ref, x_ref, g_ref, o_ref):
    o_ref[...] = x_ref[...] + g_ref[...] * _dot(a_ref[...], w_ref[...])


def _matmul(a, w_all, layer, tm, tn, out_dtype=F32):
    m, k = a.shape
    n = w_all.shape[2]
    wb = w_all.dtype.itemsize
    vm = 2 * (tm * k * 2 + k * tn * wb + tm * tn * 4) + tm * tn * 4 + k * tn * 2
    return pl.pallas_call(
        _mm_kernel,
        grid=(m // tm, n // tn),
        in_specs=[pl.BlockSpec((tm, k), lambda i, j: (i, 0)),
                  pl.BlockSpec((None, k, tn), lambda i, j: (layer, 0, j))],
        out_specs=pl.BlockSpec((tm, tn), lambda i, j: (i, j)),
        out_shape=jax.ShapeDtypeStruct((m, n), out_dtype),
        compiler_params=_cparams(("parallel", "parallel"), vm + (4 << 20)),
        name="matmul",
    )(a, w_all)


def _matmul_residual(a, w_all, layer, x, gate, tm, tn):
    m, k = a.shape
    n = w_all.shape[2]
    wb = w_all.dtype.itemsize
    vm = 2 * (tm * k * 2 + k * tn * wb + 2 * tm * tn * 4) + tm * tn * 4 + k * tn * 2
    return pl.pallas_call(
        _mm_res_kernel,
        grid=(m // tm, n // tn),
        in_specs=[pl.BlockSpec((tm, k), lambda i, j: (i, 0)),
                  pl.BlockSpec((None, k, tn), lambda i, j: (layer, 0, j)),
                  pl.BlockSpec((tm, tn), lambda i, j: (i, j)),
                  pl.BlockSpec((None, 1, tn), lambda i, j: (_cond_of_tile(i, tm), 0, j))],
        out_specs=pl.BlockSpec((tm, tn), lambda i, j: (i, j)),
        out_shape=jax.ShapeDtypeStruct((m, n), F32),
        compiler_params=_cparams(("parallel", "parallel"), vm + (4 << 20)),
        name="matmul_residual",
    )(a, w_all, x, gate)


_FFN_TF = 512


def _ffn_partial(x_ref, wa_ref, wg_ref, wd_ref):
    x = x_ref[...]
    a = _dot(x, wa_ref[...])
    g = _dot(x, wg_ref[...])
    hid = (_silu(a) * g).astype(BF16)
    return _dot(hid, wd_ref[...])


def _ffn_moe_kernel(te_ref, nu_ref, rt_ref, h_hbm, wa_ref, wg_ref, wd_ref, o_ref, stage, xb, sem):
    i = pl.program_id(0)
    f = pl.program_id(1)
    nt = pl.num_programs(0)
    nf = pl.num_programs(1)
    tm = o_ref.shape[0]
    rows = stage.shape[1]
    per = rows // (D_FF // _FFN_TF)
    slot = i % 2

    def issue(tile, to_slot, step):
        for k in range(per):
            r = step * per + k
            tok = rt_ref[tile * tm + jnp.minimum(r, tm - 1)]
            pltpu.make_async_copy(h_hbm.at[pl.ds(tok, 1), :], stage.at[to_slot, pl.ds(r, 1), :],
                                  sem.at[to_slot]).start()

    def wait(on_slot):
        pltpu.make_async_copy(h_hbm.at[pl.ds(0, rows), :], stage.at[on_slot], sem.at[on_slot]).wait()

    @pl.when((i == 0) & (f == 0))
    def _():
        def body(step, carry):
            issue(0, 0, step)
            return carry
        lax.fori_loop(0, nf, body, 0)

    @pl.when(f == 0)
    def _():
        wait(slot)
        xb[...] = stage[slot, :tm, :].astype(BF16)
        o_ref[...] = jnp.zeros(o_ref.shape, o_ref.dtype)

    nxt = jnp.minimum(i + 1, nt - 1)

    @pl.when(i < nu_ref[0])
    def _():
        issue(nxt, 1 - slot, f)
        o_ref[...] += _ffn_partial(xb, wa_ref, wg_ref, wd_ref)

    @pl.when(i >= nu_ref[0])
    def _():
        issue(nxt, 1 - slot, f)

    @pl.when((i == nt - 1) & (f == nf - 1))
    def _():
        wait(1 - slot)


def _moe_ffn_call(h, row_tok, w_gu, w_down, layer, tile_expert, n_used):
    tm = _MOE_TM
    tf = _FFN_TF
    nf = D_FF // tf
    per = -(-tm // (nf * 8)) * 8
    stage_rows = per * nf
    vm = (2 * stage_rows * D_MODEL * 4 + tm * D_MODEL * 2 + 2 * (3 * D_MODEL * tf * 2 + tm * D_MODEL * 4)
          + tm * D_MODEL * 4 + 3 * tm * tf * 4)
    return pl.pallas_call(
        _ffn_moe_kernel,
        grid_spec=pltpu.PrefetchScalarGridSpec(
            num_scalar_prefetch=3,
            grid=(_MOE_ROWS // tm, nf),
            in_specs=[pl.BlockSpec(memory_space=pl.ANY),
                      pl.BlockSpec((None, None, D_MODEL, tf), lambda i, f, te, nu, rt: (layer, te[i], 0, f)),
                      pl.BlockSpec((None, None, D_MODEL, tf), lambda i, f, te, nu, rt: (layer, te[i], 0, f + nf)),
                      pl.BlockSpec((None, None, tf, D_MODEL), lambda i, f, te, nu, rt: (layer, te[i], f, 0))],
            out_specs=pl.BlockSpec((tm, D_MODEL), lambda i, f, te, nu, rt: (i, 0)),
            scratch_shapes=[pltpu.VMEM((2, stage_rows, D_MODEL), F32), pltpu.VMEM((tm, D_MODEL), BF16),
                            pltpu.SemaphoreType.DMA((2,))]),
        out_shape=jax.ShapeDtypeStruct((_MOE_ROWS, D_MODEL), F32),
        compiler_params=_cparams(("arbitrary", "arbitrary"), vm + (4 << 20)),
        name="moe_swiglu",
    )(tile_expert, n_used, row_tok, h, w_gu, w_gu, w_down)


def _ffn_res_kernel(x_ref, wa_ref, wg_ref, wd_ref, r_ref, g_ref, o_ref):
    @pl.when(pl.program_id(1) == 0)
    def _():
        o_ref[...] = r_ref[...]

    o_ref[...] += g_ref[...] * _ffn_partial(x_ref, wa_ref, wg_ref, wd_ref)


def _dense_ffn(hb, w_gu, w_down, layer, x, gate):
    tm = 512
    tf = _FFN_TF
    nf = D_FF // tf
    vm = 2 * (tm * D_MODEL * 2 + 3 * D_MODEL * tf * 2 + 2 * tm * D_MODEL * 4) + tm * D_MODEL * 4 + 3 * tm * tf * 4
    return pl.pallas_call(
        _ffn_res_kernel,
        grid=(N_TOK // tm, nf),
        in_specs=[pl.BlockSpec((tm, D_MODEL), lambda i, f: (i, 0)),
                  pl.BlockSpec((None, D_MODEL, tf), lambda i, f: (layer, 0, f)),
                  pl.BlockSpec((None, D_MODEL, tf), lambda i, f: (layer, 0, f + nf)),
                  pl.BlockSpec((None, tf, D_MODEL), lambda i, f: (layer, f, 0)),
                  pl.BlockSpec((tm, D_MODEL), lambda i, f: (i, 0)),
                  pl.BlockSpec((None, 1, D_MODEL), lambda i, f: (_cond_of_tile(i, tm), 0, 0))],
        out_specs=pl.BlockSpec((tm, D_MODEL), lambda i, f: (i, 0)),
        out_shape=jax.ShapeDtypeStruct((N_TOK, D_MODEL), F32),
        compiler_params=_cparams(("parallel", "arbitrary"), vm + (4 << 20)),
        name="swiglu",
    )(hb, w_gu, w_gu, w_down, x, gate)


_MOE_TM = 512
_MOE_ROWS = TOP_K * N_TOK + N_EXPERTS * _MOE_TM
_GATHER_TM = 256
_DMA_ISSUE_UNROLL = 8


def _combine_kernel(d0_ref, d1_ref, y_hbm, x_ref, gt_ref, w_ref, o_ref, buf, sem):
    i = pl.program_id(0)
    tm = o_ref.shape[0]

    def issue(tile, slot):
        def body(r, carry):
            t = tile * tm + r
            pltpu.make_async_copy(y_hbm.at[pl.ds(d0_ref[t], 1), :], buf.at[slot, 0, pl.ds(r, 1), :],
                                  sem.at[slot]).start()
            pltpu.make_async_copy(y_hbm.at[pl.ds(d1_ref[t], 1), :], buf.at[slot, 1, pl.ds(r, 1), :],
                                  sem.at[slot]).start()
            return carry
        lax.fori_loop(0, tm, body, 0, unroll=_DMA_ISSUE_UNROLL)

    @pl.when(i == 0)
    def _():
        issue(0, 0)

    @pl.when(i + 1 < pl.num_programs(0))
    def _():
        issue(i + 1, (i + 1) % 2)

    slot = i % 2
    pltpu.make_async_copy(y_hbm.at[pl.ds(0, tm), :], buf.at[slot, 0], sem.at[slot]).wait()
    pltpu.make_async_copy(y_hbm.at[pl.ds(0, tm), :], buf.at[slot, 1], sem.at[slot]).wait()
    w = w_ref[...]
    y = w[:, 2:3] * buf[slot, 0] + w[:, 3:4] * buf[slot, 1]
    o_ref[...] = x_ref[...] + gt_ref[...] * y


def _combine_rows(y, d0, d1, x, gate, route):
    tm = _GATHER_TM
    return pl.pallas_call(
        _combine_kernel,
        grid_spec=pltpu.PrefetchScalarGridSpec(
            num_scalar_prefetch=2,
            grid=(N_TOK // tm,),
            in_specs=[pl.BlockSpec(memory_space=pl.ANY),
                      pl.BlockSpec((tm, D_MODEL), lambda i, a, b: (i, 0)),
                      pl.BlockSpec((None, 1, D_MODEL), lambda i, a, b: (_cond_of_tile(i, tm), 0, 0)),
                      pl.BlockSpec((tm, V7X_LANES), lambda i, a, b: (i, 0))],
            out_specs=pl.BlockSpec((tm, D_MODEL), lambda i, a, b: (i, 0)),
            scratch_shapes=[pltpu.VMEM((2, 2, tm, D_MODEL), F32), pltpu.SemaphoreType.DMA((2,))]),
        out_shape=jax.ShapeDtypeStruct((N_TOK, D_MODEL), F32),
        compiler_params=_cparams(("arbitrary",), 10 * tm * D_MODEL * 4),
        name="moe_combine",
    )(d0, d1, y, x, gate, route)


def _moe_ffn(x, norm_w, sc, sh, gate, w_router, w_gu, w_down, layer):
    h, route = _norm_router(x, norm_w, sc, sh, w_router)
    top_i = route[:, :TOP_K].astype(jnp.int32)
    member = jnp.sum(top_i[:, :, None] == jnp.arange(N_EXPERTS, dtype=jnp.int32), axis=1, dtype=jnp.int32)
    before = jnp.cumsum(member, axis=0) - member
    counts = before[-1] + member[-1]
    padded = ((counts + _MOE_TM - 1) // _MOE_TM) * _MOE_TM
    ends = jnp.cumsum(padded)
    starts = ends - padded
    dest = jnp.take(starts, top_i) + jnp.take_along_axis(before, top_i, axis=1)
    tok = jnp.broadcast_to(jnp.arange(N_TOK, dtype=jnp.int32)[:, None], (N_TOK, TOP_K))
    row_tok = jnp.zeros((_MOE_ROWS,), jnp.int32).at[dest.reshape(-1)].set(tok.reshape(-1))
    n_tiles = _MOE_ROWS // _MOE_TM
    tile_start = jnp.arange(n_tiles, dtype=jnp.int32) * _MOE_TM
    tile_expert = jnp.minimum(jnp.sum(tile_start[:, None] >= ends[None, :], axis=1), N_EXPERTS - 1).astype(jnp.int32)
    n_used = (ends[-1:] // _MOE_TM).astype(jnp.int32)
    ys = _moe_ffn_call(h, row_tok, w_gu, w_down, layer, tile_expert, n_used)
    return _combine_rows(ys, dest[:, 0], dest[:, 1], x, gate, route)


@functools.lru_cache(maxsize=None)
def _hgrn_tables():
    c = HG_CHUNK
    n_rows = c * (HG_LEVELS + 1) + HG_TOT_ROWS
    prefix = np.zeros((2, n_rows, c), np.float32)
    role = np.zeros((2, HG_LEVELS, c, 1), np.float32)
    pair = np.zeros((2, HG_LEVELS, c, c), np.float32)
    for d in range(2):
        for i in range(c):
            if d == 0:
                prefix[d, i, :i + 1] = 1.0
            else:
                prefix[d, i, i:] = 1.0
        for l in range(HG_LEVELS):
            m = 2 ** l
            for i in range(c):
                blk = i // (2 * m)
                upper = (i // m) % 2 == 1
                mid = blk * 2 * m + m
                row = c * (l + 1) + i
                if d == 0:
                    is_q = upper
                    lo, hi = (mid, i + 1) if upper else (i + 1, mid)
                else:
                    is_q = not upper
                    lo, hi = (mid, i) if upper else (i, mid)
                prefix[d, row, lo:hi] = 1.0
                role[d, l, i, 0] = 1.0 if is_q else 0.0
            for i in range(c):
                for j in range(c):
                    same = i // (2 * m) == j // (2 * m)
                    if same and role[d, l, i, 0] == 1.0 and role[d, l, j, 0] == 0.0:
                        pair[d, l, i, j] = 1.0
        prefix[d, c * (HG_LEVELS + 1):, :] = 1.0
    prefix = np.concatenate([prefix] * HG_SPLIT, axis=2)
    pair = np.concatenate([pair, pair], axis=3)
    return prefix, pair


def _pair_blockdiag(x2):
    zero = jnp.zeros((x2.shape[0], HG_DK), x2.dtype)
    return jnp.concatenate([jnp.concatenate([x2[:, :HG_DK], zero], axis=1),
                            jnp.concatenate([zero, x2[:, HG_DK:]], axis=1)], axis=0)


def _role_select(qh, k, level, backward):
    m = 2 ** level
    if m % 8 == 0:
        parts = []
        for r in range(0, HG_CHUNK, m):
            upper = (r // m) % 2 == 1
            parts.append((qh if upper != backward else k)[r:r + m])
        return jnp.concatenate(parts, axis=0)
    row = lax.broadcasted_iota(jnp.int32, qh.shape, 0)
    upper = (row // m) % 2 == 1
    return jnp.where(upper != backward, qh, k)


def _hgrn_dir_kernel(*refs, backward, hb, has_init, write_state):
    it = iter(refs)
    q_ref, f_ref, v_ref, la_ref, l1_ref, pm_ref, pair_ref = (next(it) for _ in range(7))
    s0_ref = next(it) if has_init else None
    if backward:
        of_ref, gt_ref, nw_ref = next(it), next(it), next(it)
    o_ref = next(it)
    st_ref = next(it) if write_state else None
    st_scr = next(it)

    c = HG_CHUNK
    n_chunk = HG_TB // c
    n_pair = hb // 2
    t = pl.program_id(2)

    @pl.when(t == 0)
    def _():
        for p in range(n_pair):
            if has_init:
                st_scr[p] = jnp.concatenate([s0_ref[2 * p].T, s0_ref[2 * p + 1].T], axis=1)
            else:
                st_scr[p] = jnp.zeros((HG_DV, HG_PAIR_W), F32)

    def pair_unit(p, ci):
        r0 = ci * c
        cols = slice(p * HG_PAIR_W, (p + 1) * HG_PAIR_W)
        qh = _silu(q_ref[pl.ds(r0, c), cols])
        z = f_ref[pl.ds(r0, c), cols]
        v = v_ref[pl.ds(r0, c), cols]
        la = la_ref[:, cols]
        b = l1_ref[:, cols] + (jnp.minimum(z, 0.0) - jnp.log(1.0 + jnp.exp(-jnp.abs(z))))
        g = jnp.maximum(la, b) + jnp.log(1.0 + jnp.exp(-jnp.abs(la - b)))
        k = 1.0 - jnp.exp(g)
        g1 = g.astype(BF16)
        g2 = (g - g1.astype(F32)).astype(BF16)
        sums = _dot(pm_ref[...], jnp.concatenate([g1, g2], axis=0))
        cum = sums[0:c]
        tot = sums[c * (HG_LEVELS + 1):c * (HG_LEVELS + 1) + 1]
        scores = jnp.zeros((c, HG_PAIR_W // 2), F32)
        for l in range(HG_LEVELS):
            e = jnp.exp(sums[c * (l + 1):c * (l + 2)])
            xk = (_role_select(qh, k, l, backward) * e).astype(BF16)
            scores = scores + pair_ref[l] * _dot_nt(xk, _pair_blockdiag(xk))
        vb = v.astype(BF16)
        qk = qh * k
        lane = lax.broadcasted_iota(jnp.int32, qk.shape, 1)
        self_score = jnp.where(lane < HG_DK, jnp.sum(qk[:, :HG_DK], axis=-1, keepdims=True),
                               jnp.sum(qk[:, HG_DK:], axis=-1, keepdims=True))
        o = _dot(scores.astype(BF16), _pair_blockdiag(vb)) + self_score * v
        st = st_scr[p]
        o = o + _dot_nt((qh * jnp.exp(cum)).astype(BF16), _pair_blockdiag(st.astype(BF16)))
        kt = (k * jnp.exp(tot - cum)).astype(BF16)
        v_rows = jnp.concatenate([vb[:, :HG_DV], vb[:, HG_DV:]], axis=0)
        st_scr[p] = st * jnp.exp(tot) + _dot_tn(v_rows, _pair_blockdiag(kt))
        if backward:
            o = o + of_ref[pl.ds(r0, c), cols]
            gate = _silu(gt_ref[pl.ds(r0, c), cols])
            for hh in range(2):
                hc = slice(hh * HG_DV, (hh + 1) * HG_DV)
                oh = o[:, hc]
                y = oh * lax.rsqrt(jnp.mean(oh * oh, axis=-1, keepdims=True) + EPS) * nw_ref[...]
                o_ref[pl.ds(r0, c), pl.ds(p * HG_PAIR_W + hh * HG_DV, HG_DV)] = (y * gate[:, hc]).astype(o_ref.dtype)
        else:
            o_ref[pl.ds(r0, c), cols] = o

    for i in range(n_chunk):
        ci = n_chunk - 1 - i if backward else i
        for p in range(n_pair):
            pair_unit(p, ci)

    if write_state:
        @pl.when(t == pl.num_programs(2) - 1)
        def _():
            for p in range(n_pair):
                st = st_scr[p]
                st_ref[2 * p] = st[:, :HG_DK].T
                st_ref[2 * p + 1] = st[:, HG_DK:].T


def _hgrn_scan(p, log_lb, log1m_lb, norm_w, rows0, seq, nb, hb, s0=None, write_state=False):
    w = hb * HG_DK
    ngrp = HG_HEADS // hb
    per = D_MODEL // w
    tb = HG_TB
    n_t = seq // tb
    rb0 = rows0 // tb
    prefix, pair = _hgrn_tables()
    outs = []
    o_fwd = None
    for d in range(2):
        def tblk(t, d=d):
            return n_t - 1 - t if d == 1 else t

        def col(group, d=d, tblk=tblk):
            return pl.BlockSpec((tb, w), lambda b, g, t: (rb0 + b * n_t + tblk(t), group * per + g))

        def row(d=d, tblk=tblk):
            return pl.BlockSpec((tb, w), lambda b, g, t: (b * n_t + tblk(t), g))

        in_specs = [col(0), col(1 + d), col(3),
                    pl.BlockSpec((None, 1, w), lambda b, g, t, d=d: (d, 0, g)),
                    pl.BlockSpec((None, 1, w), lambda b, g, t, d=d: (d, 0, g)),
                    pl.BlockSpec(prefix.shape[1:], lambda b, g, t: (0, 0)),
                    pl.BlockSpec(pair.shape[1:], lambda b, g, t: (0, 0, 0))]
        args = [p, p, p, log_lb, log1m_lb, jnp.asarray(prefix[d], BF16), jnp.asarray(pair[d])]
        if s0 is not None:
            in_specs.append(pl.BlockSpec((None, None, hb, HG_DK, HG_DV), lambda b, g, t, d=d: (b, d, g, 0, 0)))
            args.append(s0)
        if d == 1:
            in_specs += [row(), col(4), pl.BlockSpec((1, HG_DV), lambda b, g, t: (0, 0))]
            args += [o_fwd, p, norm_w.reshape(1, HG_DV)]
        out_specs = [row()]
        out_shape = [jax.ShapeDtypeStruct((nb * seq, D_MODEL), BF16 if d == 1 else F32)]
        if write_state:
            out_specs.append(pl.BlockSpec((None, hb, HG_DK, HG_DV), lambda b, g, t: (b, g, 0, 0)))
            out_shape.append(jax.ShapeDtypeStruct((nb, HG_HEADS, HG_DK, HG_DV), F32))
        vm = 2 * 7 * tb * w * 4 + 6 * hb * HG_DK * HG_DV * 4 + (16 << 20)
        res = pl.pallas_call(
            functools.partial(_hgrn_dir_kernel, backward=d == 1, hb=hb, has_init=s0 is not None,
                              write_state=write_state),
            grid=(nb, ngrp, n_t),
            in_specs=in_specs,
            out_specs=out_specs,
            out_shape=out_shape,
            scratch_shapes=[pltpu.VMEM((hb // 2, HG_DV, HG_PAIR_W), F32)],
            compiler_params=_cparams(("parallel", "parallel", "arbitrary"), vm),
            name="hgrn_bwd" if d == 1 else "hgrn_fwd",
        )(*args)
        if d == 0:
            o_fwd = res[0]
        outs.append(res)
    o = outs[1][0]
    if write_state:
        return o, jnp.stack([outs[0][1], outs[1][1]], axis=1)
    return (o,)


def _hgrn_mixer(hbf, w_in, w_out, layer, lb, norm_w, s0_lat, x, gate):
    p = _matmul(hbf, w_in, layer, _MM_TM, _MM_TN)
    log_lb = jnp.log(lb).reshape(2, 1, HG_FDIM)
    log1m_lb = jnp.log1p(-lb).reshape(2, 1, HG_FDIM)
    o_ctx, st = _hgrn_scan(p, log_lb, log1m_lb, norm_w, 0, SEQ, BATCH, 8, write_state=True)
    (o_lat,) = _hgrn_scan(p, log_lb, log1m_lb, norm_w, N_CTX, DEC_SEQ, DEC_BATCH, 8, s0=s0_lat)
    o = jnp.concatenate([o_ctx, o_lat], axis=0)
    return _matmul_residual(o, w_out, layer, x, gate, _MM_TM, _MM_TN), st


def _conv_kernel(b_ref, c_ref, x_ref, w_ref, o_ref):
    u = c_ref[...] * x_ref[...]
    seq = u.shape[0]
    row = lax.broadcasted_iota(jnp.int32, u.shape, 0)
    prev = jnp.where(row == 0, 0.0, pltpu.roll(u, 1, 0))
    nxt = jnp.where(row == seq - 1, 0.0, pltpu.roll(u, seq - 1, 0))
    w = w_ref[...]
    y = prev * w[0:1] + u * w[1:2] + nxt * w[2:3]
    o_ref[...] = (b_ref[...] * y).astype(o_ref.dtype)


def _conv_gate(p, conv_w, rows0, seq, nb):
    tc = 256
    per = D_MODEL // tc
    rb0 = rows0 // seq

    def col(group):
        return pl.BlockSpec((seq, tc), lambda b, j: (rb0 + b, group * per + j))

    return pl.pallas_call(
        _conv_kernel,
        grid=(nb, per),
        in_specs=[col(0), col(1), col(2), pl.BlockSpec((CONV_W, tc), lambda b, j: (0, j))],
        out_specs=pl.BlockSpec((seq, tc), lambda b, j: (b, j)),
        out_shape=jax.ShapeDtypeStruct((nb * seq, D_MODEL), BF16),
        compiler_params=_cparams(("parallel", "parallel"), 14 * seq * tc * 4),
        name="conv_gate",
    )(p, p, p, conv_w)


def _conv_mixer(hbf, w_in, w_out, layer, conv_w, x, gate):
    p = _matmul(hbf, w_in, layer, _MM_TM, _MM_TN)
    o = jnp.concatenate([_conv_gate(p, conv_w, 0, SEQ, BATCH),
                         _conv_gate(p, conv_w, N_CTX, DEC_SEQ, DEC_BATCH)], axis=0)
    return _matmul_residual(o, w_out, layer, x, gate, _MM_TM, _MM_TN)


def _head_norm(x, w):
    return x * lax.rsqrt(jnp.mean(x * x, axis=-1, keepdims=True) + EPS) * w


def _rope(x, cos, sin_signed):
    lane = lax.broadcasted_iota(jnp.int32, x.shape, 1)
    first = (lane % ROPE_AXIS_DIM) < (ROPE_AXIS_DIM // 2)
    rot = jnp.where(first, pltpu.roll(x, HEAD_DIM - ROPE_AXIS_DIM // 2, 1), pltpu.roll(x, ROPE_AXIS_DIM // 2, 1))
    return x * cos + rot * sin_signed


def _kvprep_kernel(*refs, rope):
    if rope:
        k_ref, v_ref, w_ref, cos_ref, sin_ref, kb_ref, vb_ref = refs
    else:
        k_ref, v_ref, w_ref, kn_ref, vf_ref, kb_ref, vb_ref = refs
        vf_ref[...] = v_ref[...]
    vb_ref[...] = v_ref[...].astype(BF16)
    for h in range(N_KV_HEADS):
        cols = slice(h * HEAD_DIM, (h + 1) * HEAD_DIM)
        kn = _head_norm(k_ref[:, cols], w_ref[...])
        if rope:
            kn = _rope(kn, cos_ref[...], sin_ref[...])
        else:
            kn_ref[:, cols] = kn
        kb_ref[:, cols] = kn.astype(BF16)


def _kv_prep(qkv, k_norm, rows0, nrows, rope_tabs):
    tm = 512
    kw = N_KV_HEADS * HEAD_DIM
    kcol = (N_HEADS * HEAD_DIM) // kw
    rb0 = rows0 // tm
    in_specs = [pl.BlockSpec((tm, kw), lambda i: (rb0 + i, kcol)),
                pl.BlockSpec((tm, kw), lambda i: (rb0 + i, kcol + 1)),
                pl.BlockSpec((1, HEAD_DIM), lambda i: (0, 0))]
    args = [qkv, qkv, k_norm.reshape(1, HEAD_DIM)]
    o_spec = pl.BlockSpec((tm, kw), lambda i: (i, 0))
    out_specs = [o_spec, o_spec]
    out_shape = [jax.ShapeDtypeStruct((nrows, kw), BF16)] * 2
    if rope_tabs is not None:
        per = DEC_SEQ // tm
        in_specs += [pl.BlockSpec((tm, HEAD_DIM), lambda i: (i % per, 0))] * 2
        args += list(rope_tabs)
    else:
        out_specs = [o_spec, o_spec] + out_specs
        out_shape = [jax.ShapeDtypeStruct((nrows, kw), F32)] * 2 + out_shape
    return pl.pallas_call(
        functools.partial(_kvprep_kernel, rope=rope_tabs is not None),
        grid=(nrows // tm,),
        in_specs=in_specs,
        out_specs=out_specs,
        out_shape=out_shape,
        compiler_params=_cparams(("parallel",), 16 * tm * kw * 4),
        name="kv_prep",
    )(*args)


def _attn_kernel(*refs, rope):
    if rope:
        q_ref, k_ref, v_ref, w_ref, cos_ref, sin_ref, o_ref = refs
    else:
        q_ref, k_ref, v_ref, w_ref, o_ref = refs
    kk = k_ref[...]
    vv = v_ref[...]
    scale = HEAD_DIM ** -0.5
    for g in range(GQA_GROUP):
        cols = slice(g * HEAD_DIM, (g + 1) * HEAD_DIM)
        q = _head_norm(q_ref[:, cols], w_ref[...])
        if rope:
            q = _rope(q, cos_ref[...], sin_ref[...])
        s = _dot_nt(q.astype(BF16), kk) * scale
        p = jnp.exp(s - jnp.max(s, axis=-1, keepdims=True))
        den = jnp.sum(p, axis=-1, keepdims=True)
        o = _dot(p.astype(BF16), vv) / den
        o_ref[:, cols] = o.astype(o_ref.dtype)


def _attention(qkv, kb, vb, q_norm, rows0, seq, nb, tk, rope_tabs):
    tq = 256
    gw = GQA_GROUP * HEAD_DIM
    nq = seq // tq
    rb0 = rows0 // tq
    in_specs = [pl.BlockSpec((tq, gw), lambda b, h, i: (rb0 + b * nq + i, h)),
                pl.BlockSpec((tk, HEAD_DIM), lambda b, h, i: (b, h)),
                pl.BlockSpec((tk, HEAD_DIM), lambda b, h, i: (b, h)),
                pl.BlockSpec((1, HEAD_DIM), lambda b, h, i: (0, 0))]
    args = [qkv, kb, vb, q_norm.reshape(1, HEAD_DIM)]
    if rope_tabs is not None:
        in_specs += [pl.BlockSpec((tq, HEAD_DIM), lambda b, h, i: (i, 0))] * 2
        args += list(rope_tabs)
    vm = 2 * (tq * gw * 4 + 2 * tk * HEAD_DIM * 2 + tq * gw * 2) + 4 * tq * tk * 4 + (4 << 20)
    return pl.pallas_call(
        functools.partial(_attn_kernel, rope=rope_tabs is not None),
        grid=(nb, N_KV_HEADS, nq),
        in_specs=in_specs,
        out_specs=pl.BlockSpec((tq, gw), lambda b, h, i: (b * nq + i, h)),
        out_shape=jax.ShapeDtypeStruct((nb * seq, D_MODEL), BF16),
        compiler_params=_cparams(("parallel", "parallel", "parallel"), vm),
        name="attention",
    )(*args)


def _rope_tables():
    n_rows = DEC_SEQ // GRID_W
    row = jnp.repeat(jnp.arange(n_rows), GRID_W).astype(F32)
    colp = jnp.tile(jnp.arange(GRID_W), n_rows).astype(F32)
    inv = ROPE_THETA ** (-jnp.arange(0, ROPE_AXIS_DIM, 2, dtype=F32) / ROPE_AXIS_DIM)
    ang_r = row[:, None] * inv
    ang_c = colp[:, None] * inv
    ang = jnp.concatenate([ang_r, ang_r, ang_c, ang_c], axis=-1)
    quarter = ROPE_AXIS_DIM // 2
    sign = jnp.where((jnp.arange(HEAD_DIM) % ROPE_AXIS_DIM) < quarter, -1.0, 1.0).astype(F32)
    return jnp.cos(ang), jnp.sin(ang) * sign


def _attn_mixer(hbf, w_qkv, w_out, layer, q_norm, k_norm, cache_k, cache_v, x, gate):
    qkv = _matmul(hbf, w_qkv, layer, _MM_TM, _MM_TN)
    kw = N_KV_HEADS * HEAD_DIM
    tabs = _rope_tables()
    kn_ctx, v_ctx, kb_ctx, vb_ctx = _kv_prep(qkv, k_norm, 0, N_CTX, None)
    kb_lat, vb_lat = _kv_prep(qkv, k_norm, N_CTX, N_LAT, tabs)
    o_ctx = _attention(qkv, kb_ctx, vb_ctx, q_norm, 0, SEQ, BATCH, SEQ, None)
    tk = PAST_LEN + DEC_SEQ
    keys = jnp.concatenate([cache_k.reshape(DEC_BATCH, PAST_LEN, kw).astype(BF16),
                            kb_lat.reshape(DEC_BATCH, DEC_SEQ, kw)], axis=1).reshape(DEC_BATCH * tk, kw)
    vals = jnp.concatenate([cache_v.reshape(DEC_BATCH, PAST_LEN, kw).astype(BF16),
                            vb_lat.reshape(DEC_BATCH, DEC_SEQ, kw)], axis=1).reshape(DEC_BATCH * tk, kw)
    o_lat = _attention(qkv, keys, vals, q_norm, N_CTX, DEC_SEQ, DEC_BATCH, tk, tabs)
    o = jnp.concatenate([o_ctx, o_lat], axis=0)
    new_k = kn_ctx.reshape(BATCH, SEQ, N_KV_HEADS, HEAD_DIM)
    new_v = v_ctx.reshape(BATCH, SEQ, N_KV_HEADS, HEAD_DIM)
    return _matmul_residual(o, w_out, layer, x, gate, _MM_TM, _MM_TN), new_k, new_v


def kernel(x_prompt, x_sample, state_hgrn, cache_k, cache_v, c, c_ctx, ada_w, ada_b, norm_w, final_norm_w,
           hgrn_w_in, hgrn_lb_logits, hgrn_norm_w, hgrn_w_out, conv_w_in, conv_w, conv_w_out,
           attn_w_qkv, attn_q_norm, attn_k_norm, attn_w_out, ffn_w_gate_up, ffn_w_down,
           moe_w_router, moe_w_gate_up, moe_w_down):
    x = jnp.concatenate([x_prompt.reshape(N_CTX, D_MODEL), x_sample.reshape(N_LAT, D_MODEL)], axis=0)
    conds = jnp.zeros((COND_PAD, D_MODEL), F32).at[0].set(c_ctx).at[1:N_COND].set(c)
    mod = _modulation(conds, ada_w, ada_b)
    mod = mod.reshape(DEPTH, COND_PAD, 6, 1, D_MODEL).transpose(0, 2, 1, 3, 4)

    probs = jax.nn.softmax(hgrn_lb_logits.astype(F32), axis=0)
    csum = jnp.cumsum(probs, axis=0)
    lower_bounds = csum - csum[:1]

    ffn_gu, ffn_down = ffn_w_gate_up.astype(BF16), ffn_w_down.astype(BF16)
    moe_gu, moe_down = moe_w_gate_up.astype(BF16), moe_w_down.astype(BF16)
    hgrn_w_in, hgrn_w_out = hgrn_w_in.astype(BF16), hgrn_w_out.astype(BF16)
    conv_w_in, conv_w_out = conv_w_in.astype(BF16), conv_w_out.astype(BF16)
    attn_w_qkv, attn_w_out = attn_w_qkv.astype(BF16), attn_w_out.astype(BF16)

    new_states, new_k, new_v = [], None, None
    for layer in range(DEPTH):
        kind = layer % N_MIXERS
        j = layer // N_MIXERS
        sh1, sc1, g1, sh2, sc2, g2 = (mod[layer, t] for t in range(6))
        hbf = _norm_mod(x, norm_w[layer, 0], sc1, sh1)
        if kind == 0:
            x, st = _hgrn_mixer(hbf, hgrn_w_in, hgrn_w_out, j, lower_bounds[j], hgrn_norm_w[j],
                                state_hgrn[:, j], x, g1)
            new_states.append(st)
        elif kind == 1:
            x = _conv_mixer(hbf, conv_w_in, conv_w_out, j, conv_w[j], x, g1)
        else:
            x, new_k, new_v = _attn_mixer(hbf, attn_w_qkv, attn_w_out, j, attn_q_norm[j], attn_k_norm[j],
                                          cache_k[:, j], cache_v[:, j], x, g1)
        f = layer // 2
        if layer % 2 == 0:
            hbf = _norm_mod(x, norm_w[layer, 1], sc2, sh2)
            x = _dense_ffn(hbf, ffn_gu, ffn_down, f, x, g2)
        else:
            x = _moe_ffn(x, norm_w[layer, 1], sc2, sh2, g2, moe_w_router[f], moe_gu, moe_down, f)
    y_prompt = _final_norm(x, final_norm_w, 0, N_CTX).reshape(BATCH, SEQ, D_MODEL)
    y_sample = _final_norm(x, final_norm_w, N_CTX, N_LAT).reshape(DEC_BATCH, DEC_SEQ, D_MODEL)
    new_state_hgrn = jnp.stack(new_states, axis=1)
    return (y_prompt, y_sample, new_state_hgrn, new_k[:, None], new_v[:, None])
```

```python
import functools
import math

import numpy as np
import jax
import jax.numpy as jnp
from jax import lax
from jax.experimental import pallas as pl
from jax.experimental.pallas import tpu as pltpu

F32 = jnp.float32
BF16 = jnp.bfloat16

D_MODEL = 2048
BATCH = 32
SEQ = 256
DEPTH = 4
DEC_BATCH = 8
DEC_SEQ = 2048
PAST_LEN = 512
GRID_W = 64
N_MIXERS = 3
EPS = 1e-6
HG_HEADS = 16
HG_DK = 128
HG_DV = 128
HG_FDIM = HG_HEADS * HG_DK
CONV_W = 3
HEAD_DIM = 128
N_HEADS = 16
N_KV_HEADS = 4
GQA_GROUP = 4
ROPE_THETA = 10000.0
ROPE_AXIS_DIM = HEAD_DIM // 2
D_FF = 5632
N_EXPERTS = 8
TOP_K = 2

N_CTX = BATCH * SEQ
N_LAT = DEC_BATCH * DEC_SEQ
N_TOK = N_CTX + N_LAT
N_COND = 1 + DEC_BATCH
COND_PAD = 16

V7X_LANES = 128
LANE_CHUNKS = D_MODEL // V7X_LANES
V7X_VMEM_BYTES = 64 * 1024 * 1024
VMEM_CAP = 56 * 1024 * 1024

HG_CHUNK = 64
HG_LEVELS = int(math.log2(HG_CHUNK))
HG_TOT_ROWS = 16
HG_SPLIT = 2
HG_PAIR_W = 2 * HG_DK
HG_TB = 256


def _cparams(sem, vmem_bytes):
    return pltpu.CompilerParams(dimension_semantics=sem,
                                vmem_limit_bytes=int(min(VMEM_CAP, max(vmem_bytes, 16 * 1024 * 1024))))


def _cond_of_tile(i, tm):
    r = i * tm
    return jnp.where(r < N_CTX, 0, 1 + (r - N_CTX) // DEC_SEQ)


def _silu(x):
    return x * jax.nn.sigmoid(x)


def _dot(a, b):
    return jnp.dot(a, b, preferred_element_type=F32)


def _dot_nt(a, b):
    return lax.dot_general(a, b, (((1,), (1,)), ((), ())), preferred_element_type=F32)


def _dot_tn(a, b):
    return lax.dot_general(a, b, (((0,), (0,)), ((), ())), preferred_element_type=F32)


def _mod_kernel(c_ref, w_ref, b_ref, o_ref):
    a = _silu(c_ref[...]).astype(BF16)
    o_ref[...] = _dot(a, w_ref[...].astype(BF16)) + b_ref[...]


def _modulation(conds, ada_w, ada_b):
    tn = 1024
    n_out = 6 * D_MODEL
    return pl.pallas_call(
        _mod_kernel,
        grid=(DEPTH, n_out // tn),
        in_specs=[pl.BlockSpec((COND_PAD, D_MODEL), lambda l, j: (0, 0)),
                  pl.BlockSpec((None, D_MODEL, tn), lambda l, j: (l, 0, j)),
                  pl.BlockSpec((None, 1, tn), lambda l, j: (l, 0, j))],
        out_specs=pl.BlockSpec((None, COND_PAD, tn), lambda l, j: (l, 0, j)),
        out_shape=jax.ShapeDtypeStruct((DEPTH, COND_PAD, n_out), F32),
        compiler_params=_cparams(("parallel", "parallel"), 3 * D_MODEL * tn * 4 + (4 << 20)),
        name="modulation",
    )(conds, ada_w, ada_b.reshape(DEPTH, 1, n_out))


def _norm_body(x_ref, w_ref, sc_ref, sh_ref):
    x = x_ref[...]
    y = x * lax.rsqrt(jnp.mean(x * x, axis=-1, keepdims=True) + EPS) * w_ref[...]
    return y * (1.0 + sc_ref[...]) + sh_ref[...]


def _norm_mod_kernel(x_ref, w_ref, sc_ref, sh_ref, o_ref):
    o_ref[...] = _norm_body(x_ref, w_ref, sc_ref, sh_ref).astype(o_ref.dtype)


def _final_norm_kernel(x_ref, w_ref, o_ref):
    x = x_ref[...]
    o_ref[...] = x * lax.rsqrt(jnp.mean(x * x, axis=-1, keepdims=True) + EPS) * w_ref[...]


def _norm_router_kernel(x_ref, w_ref, sc_ref, sh_ref, rh_ref, rl_ref, h_ref, r_ref):
    h = _norm_body(x_ref, w_ref, sc_ref, sh_ref)
    tm = h.shape[0]
    for cc in range(LANE_CHUNKS):
        h_ref[pl.ds(cc, tm, stride=LANE_CHUNKS), :] = h[:, cc * V7X_LANES:(cc + 1) * V7X_LANES]
    hh = h.astype(BF16)
    hl = (h - hh.astype(F32)).astype(BF16)
    logits = _dot(hh, rh_ref[...]) + (_dot(hh, rl_ref[...]) + _dot(hl, rh_ref[...]))
    lane = lax.broadcasted_iota(jnp.int32, logits.shape, 1)
    neg = jnp.float32(-jnp.inf)
    logits = jnp.where(lane < N_EXPERTS, logits, neg)
    v1 = jnp.max(logits, axis=-1, keepdims=True)
    i1 = jnp.min(jnp.where(logits == v1, lane, V7X_LANES), axis=-1, keepdims=True)
    rest = jnp.where(lane == i1, neg, logits)
    v2 = jnp.max(rest, axis=-1, keepdims=True)
    i2 = jnp.min(jnp.where(rest == v2, lane, V7X_LANES), axis=-1, keepdims=True)
    e = jnp.exp(v2 - v1)
    g1 = 1.0 / (1.0 + e)
    g2 = e * g1
    out = jnp.where(lane == 0, i1.astype(F32), 0.0)
    out = jnp.where(lane == 1, i2.astype(F32), out)
    out = jnp.where(lane == 2, g1, out)
    out = jnp.where(lane == 3, g2, out)
    r_ref[...] = out


_NORM_TM = 512


def _row_specs(tm):
    x_spec = pl.BlockSpec((tm, D_MODEL), lambda i: (i, 0))
    w_spec = pl.BlockSpec((1, D_MODEL), lambda i: (0, 0))
    c_spec = pl.BlockSpec((None, 1, D_MODEL), lambda i: (_cond_of_tile(i, tm), 0, 0))
    return x_spec, w_spec, c_spec


def _norm_mod(x, w, sc, sh, out_dtype=BF16):
    tm = _NORM_TM
    x_spec, w_spec, c_spec = _row_specs(tm)
    return pl.pallas_call(
        _norm_mod_kernel,
        grid=(N_TOK // tm,),
        in_specs=[x_spec, w_spec, c_spec, c_spec],
        out_specs=x_spec,
        out_shape=jax.ShapeDtypeStruct((N_TOK, D_MODEL), out_dtype),
        compiler_params=_cparams(("parallel",), 6 * tm * D_MODEL * 4),
        name="norm_mod",
    )(x, w.reshape(1, D_MODEL), sc, sh)


def _final_norm(x, w, rows0, nrows):
    tm = _NORM_TM
    rb0 = rows0 // tm
    return pl.pallas_call(
        _final_norm_kernel,
        grid=(nrows // tm,),
        in_specs=[pl.BlockSpec((tm, D_MODEL), lambda i: (rb0 + i, 0)),
                  pl.BlockSpec((1, D_MODEL), lambda i: (0, 0))],
        out_specs=pl.BlockSpec((tm, D_MODEL), lambda i: (i, 0)),
        out_shape=jax.ShapeDtypeStruct((nrows, D_MODEL), F32),
        compiler_params=_cparams(("parallel",), 6 * tm * D_MODEL * 4),
        name="final_norm",
    )(x, w.reshape(1, D_MODEL))


def _norm_router(x, w, sc, sh, w_router):
    tm = _NORM_TM
    x_spec, w_spec, c_spec = _row_specs(tm)
    wr = jnp.zeros((D_MODEL, V7X_LANES), F32).at[:, :N_EXPERTS].set(w_router)
    wr_hi = wr.astype(BF16)
    wr_lo = (wr - wr_hi.astype(F32)).astype(BF16)
    r_spec = pl.BlockSpec((D_MODEL, V7X_LANES), lambda i: (0, 0))
    return pl.pallas_call(
        _norm_router_kernel,
        grid=(N_TOK // tm,),
        in_specs=[x_spec, w_spec, c_spec, c_spec, r_spec, r_spec],
        out_specs=[pl.BlockSpec((tm * LANE_CHUNKS, V7X_LANES), lambda i: (i, 0)),
                   pl.BlockSpec((tm, V7X_LANES), lambda i: (i, 0))],
        out_shape=[jax.ShapeDtypeStruct((N_TOK * LANE_CHUNKS, V7X_LANES), F32),
                   jax.ShapeDtypeStruct((N_TOK, V7X_LANES), F32)],
        compiler_params=_cparams(("parallel",), 8 * tm * D_MODEL * 4),
        name="norm_router",
    )(x, w.reshape(1, D_MODEL), sc, sh, wr_hi, wr_lo)


_MM_TM = 1024
_MM_TN = 1024


def _mm_kernel(a_ref, w_ref, o_ref):
    o_ref[...] = _dot(a_ref[...], w_ref[...]).astype(o_ref.dtype)


def _mm_res_kernel(a_ref, w_ref, x_ref, g_ref, o_ref):
    o_ref[...] = x_ref[...] + g_ref[...] * _dot(a_ref[...], w_ref[...])


def _matmul(a, w_all, layer, tm, tn, out_dtype=F32):
    m, k = a.shape
    n = w_all.shape[2]
    wb = w_all.dtype.itemsize
    vm = 2 * (tm * k * 2 + k * tn * wb + tm * tn * 4) + tm * tn * 4 + k * tn * 2
    return pl.pallas_call(
        _mm_kernel,
        grid=(m // tm, n // tn),
        in_specs=[pl.BlockSpec((tm, k), lambda i, j: (i, 0)),
                  pl.BlockSpec((None, k, tn), lambda i, j: (layer, 0, j))],
        out_specs=pl.BlockSpec((tm, tn), lambda i, j: (i, j)),
        out_shape=jax.ShapeDtypeStruct((m, n), out_dtype),
        compiler_params=_cparams(("parallel", "parallel"), vm + (4 << 20)),
        name="matmul",
    )(a, w_all)


def _matmul_residual(a, w_all, layer, x, gate, tm, tn):
    m, k = a.shape
    n = w_all.shape[2]
    wb = w_all.dtype.itemsize
    vm = 2 * (tm * k * 2 + k * tn * wb + 2 * tm * tn * 4) + tm * tn * 4 + k * tn * 2
    return pl.pallas_call(
        _mm_res_kernel,
        grid=(m // tm, n // tn),
        in_specs=[pl.BlockSpec((tm, k), lambda i, j: (i, 0)),
                  pl.BlockSpec((None, k, tn), lambda i, j: (layer, 0, j)),
                  pl.BlockSpec((tm, tn), lambda i, j: (i, j)),
                  pl.BlockSpec((None, 1, tn), lambda i, j: (_cond_of_tile(i, tm), 0, j))],
        out_specs=pl.BlockSpec((tm, tn), lambda i, j: (i, j)),
        out_shape=jax.ShapeDtypeStruct((m, n), F32),
        compiler_params=_cparams(("parallel", "parallel"), vm + (4 << 20)),
        name="matmul_residual",
    )(a, w_all, x, gate)


_FFN_TF = 512


def _ffn_partial(x_ref, wa_ref, wg_ref, wd_ref):
    x = x_ref[...]
    a = _dot(x, wa_ref[...])
    g = _dot(x, wg_ref[...])
    hid = (_silu(a) * g).astype(BF16)
    return _dot(hid, wd_ref[...])


def _ffn_moe_kernel(te_ref, nu_ref, rt_ref, h_hbm, wa_ref, wg_ref, wd_ref, o_ref, stage, xb, sem):
    i = pl.program_id(0)
    f = pl.program_id(1)
    nt = pl.num_programs(0)
    nf = pl.num_programs(1)
    tm = o_ref.shape[0]
    lc = LANE_CHUNKS
    rows = stage.shape[1] // lc
    per = rows // (D_FF // _FFN_TF)
    slot = i % 2

    def issue(tile, to_slot, step):
        for k in range(per):
            r = step * per + k
            tok = rt_ref[tile * tm + jnp.minimum(r, tm - 1)]
            src = h_hbm.at[pl.ds(pl.multiple_of(tok * lc, lc), lc), :]
            dst = stage.at[to_slot, pl.ds(pl.multiple_of(r * lc, lc), lc), :]
            pltpu.make_async_copy(src, dst, sem.at[to_slot]).start()

    def wait(on_slot):
        pltpu.make_async_copy(h_hbm.at[pl.ds(0, rows * lc), :], stage.at[on_slot], sem.at[on_slot]).wait()

    @pl.when((i == 0) & (f == 0))
    def _():
        def body(step, carry):
            issue(0, 0, step)
            return carry
        lax.fori_loop(0, nf, body, 0)

    @pl.when(f == 0)
    def _():
        wait(slot)
        for cc in range(lc):
            xb[:, cc * V7X_LANES:(cc + 1) * V7X_LANES] = stage[slot, pl.ds(cc, tm, stride=lc), :].astype(BF16)
        o_ref[...] = jnp.zeros(o_ref.shape, o_ref.dtype)

    nxt = jnp.minimum(i + 1, nt - 1)

    @pl.when(i < nu_ref[0])
    def _():
        issue(nxt, 1 - slot, f)
        o_ref[...] += _ffn_partial(xb, wa_ref, wg_ref, wd_ref)

    @pl.when(i >= nu_ref[0])
    def _():
        issue(nxt, 1 - slot, f)

    @pl.when((i == nt - 1) & (f == nf - 1))
    def _():
        wait(1 - slot)


def _moe_ffn_call(h, row_tok, w_gu, w_down, layer, tile_expert, n_used):
    tm = _MOE_TM
    tf = _FFN_TF
    nf = D_FF // tf
    per = -(-tm // (nf * 8)) * 8
    stage_rows = per * nf
    vm = (2 * stage_rows * D_MODEL * 4 + tm * D_MODEL * 2 + 2 * (3 * D_MODEL * tf * 2 + tm * D_MODEL * 4)
          + tm * D_MODEL * 4 + 3 * tm * tf * 4)
    return pl.pallas_call(
        _ffn_moe_kernel,
        grid_spec=pltpu.PrefetchScalarGridSpec(
            num_scalar_prefetch=3,
            grid=(_MOE_ROWS // tm, nf),
            in_specs=[pl.BlockSpec(memory_space=pl.ANY),
                      pl.BlockSpec((None, None, D_MODEL, tf), lambda i, f, te, nu, rt: (layer, te[i], 0, f)),
                      pl.BlockSpec((None, None, D_MODEL, tf), lambda i, f, te, nu, rt: (layer, te[i], 0, f + nf)),
                      pl.BlockSpec((None, None, tf, D_MODEL), lambda i, f, te, nu, rt: (layer, te[i], f, 0))],
            out_specs=pl.BlockSpec((tm, D_MODEL), lambda i, f, te, nu, rt: (i, 0)),
            scratch_shapes=[pltpu.VMEM((2, stage_rows * LANE_CHUNKS, V7X_LANES), F32),
                            pltpu.VMEM((tm, D_MODEL), BF16),
                            pltpu.SemaphoreType.DMA((2,))]),
        out_shape=jax.ShapeDtypeStruct((_MOE_ROWS, D_MODEL), F32),
        compiler_params=_cparams(("arbitrary", "arbitrary"), vm + (4 << 20)),
        name="moe_swiglu",
    )(tile_expert, n_used, row_tok, h, w_gu, w_gu, w_down)


def _ffn_res_kernel(x_ref, wa_ref, wg_ref, wd_ref, r_ref, g_ref, o_ref):
    @pl.when(pl.program_id(1) == 0)
    def _():
        o_ref[...] = r_ref[...]

    o_ref[...] += g_ref[...] * _ffn_partial(x_ref, wa_ref, wg_ref, wd_ref)


def _dense_ffn(hb, w_gu, w_down, layer, x, gate):
    tm = 512
    tf = _FFN_TF
    nf = D_FF // tf
    vm = 2 * (tm * D_MODEL * 2 + 3 * D_MODEL * tf * 2 + 2 * tm * D_MODEL * 4) + tm * D_MODEL * 4 + 3 * tm * tf * 4
    return pl.pallas_call(
        _ffn_res_kernel,
        grid=(N_TOK // tm, nf),
        in_specs=[pl.BlockSpec((tm, D_MODEL), lambda i, f: (i, 0)),
                  pl.BlockSpec((None, D_MODEL, tf), lambda i, f: (layer, 0, f)),
                  pl.BlockSpec((None, D_MODEL, tf), lambda i, f: (layer, 0, f + nf)),
                  pl.BlockSpec((None, tf, D_MODEL), lambda i, f: (layer, f, 0)),
                  pl.BlockSpec((tm, D_MODEL), lambda i, f: (i, 0)),
                  pl.BlockSpec((None, 1, D_MODEL), lambda i, f: (_cond_of_tile(i, tm), 0, 0))],
        out_specs=pl.BlockSpec((tm, D_MODEL), lambda i, f: (i, 0)),
        out_shape=jax.ShapeDtypeStruct((N_TOK, D_MODEL), F32),
        compiler_params=_cparams(("parallel", "arbitrary"), vm + (4 << 20)),
        name="swiglu",
    )(hb, w_gu, w_gu, w_down, x, gate)


_MOE_TM = 512
_MOE_ROWS = TOP_K * N_TOK + N_EXPERTS * _MOE_TM
_GATHER_TM = 256
_DMA_ISSUE_UNROLL = 8


def _combine_kernel(d0_ref, d1_ref, y_hbm, x_ref, gt_ref, w_ref, o_ref, buf, sem):
    i = pl.program_id(0)
    tm = o_ref.shape[0]

    def issue(tile, slot):
        def body(r, carry):
            t = tile * tm + r
            pltpu.make_async_copy(y_hbm.at[pl.ds(d0_ref[t], 1), :], buf.at[slot, 0, pl.ds(r, 1), :],
                                  sem.at[slot]).start()
            pltpu.make_async_copy(y_hbm.at[pl.ds(d1_ref[t], 1), :], buf.at[slot, 1, pl.ds(r, 1), :],
                                  sem.at[slot]).start()
            return carry
        lax.fori_loop(0, tm, body, 0, unroll=_DMA_ISSUE_UNROLL)

    @pl.when(i == 0)
    def _():
        issue(0, 0)

    @pl.when(i + 1 < pl.num_programs(0))
    def _():
        issue(i + 1, (i + 1) % 2)

    slot = i % 2
    pltpu.make_async_copy(y_hbm.at[pl.ds(0, tm), :], buf.at[slot, 0], sem.at[slot]).wait()
    pltpu.make_async_copy(y_hbm.at[pl.ds(0, tm), :], buf.at[slot, 1], sem.at[slot]).wait()
    w = w_ref[...]
    y = w[:, 2:3] * buf[slot, 0] + w[:, 3:4] * buf[slot, 1]
    o_ref[...] = x_ref[...] + gt_ref[...] * y


def _combine_rows(y, d0, d1, x, gate, route):
    tm = _GATHER_TM
    return pl.pallas_call(
        _combine_kernel,
        grid_spec=pltpu.PrefetchScalarGridSpec(
            num_scalar_prefetch=2,
            grid=(N_TOK // tm,),
            in_specs=[pl.BlockSpec(memory_space=pl.ANY),
                      pl.BlockSpec((tm, D_MODEL), lambda i, a, b: (i, 0)),
                      pl.BlockSpec((None, 1, D_MODEL), lambda i, a, b: (_cond_of_tile(i, tm), 0, 0)),
                      pl.BlockSpec((tm, V7X_LANES), lambda i, a, b: (i, 0))],
            out_specs=pl.BlockSpec((tm, D_MODEL), lambda i, a, b: (i, 0)),
            scratch_shapes=[pltpu.VMEM((2, 2, tm, D_MODEL), F32), pltpu.SemaphoreType.DMA((2,))]),
        out_shape=jax.ShapeDtypeStruct((N_TOK, D_MODEL), F32),
        compiler_params=_cparams(("arbitrary",), 10 * tm * D_MODEL * 4),
        name="moe_combine",
    )(d0, d1, y, x, gate, route)


def _moe_ffn(x, norm_w, sc, sh, gate, w_router, w_gu, w_down, layer):
    h, route = _norm_router(x, norm_w, sc, sh, w_router)
    top_i = route[:, :TOP_K].astype(jnp.int32)
    member = jnp.sum(top_i[:, :, None] == jnp.arange(N_EXPERTS, dtype=jnp.int32), axis=1, dtype=jnp.int32)
    before = jnp.cumsum(member, axis=0) - member
    counts = before[-1] + member[-1]
    padded = ((counts + _MOE_TM - 1) // _MOE_TM) * _MOE_TM
    ends = jnp.cumsum(padded)
    starts = ends - padded
    dest = jnp.take(starts, top_i) + jnp.take_along_axis(before, top_i, axis=1)
    tok = jnp.broadcast_to(jnp.arange(N_TOK, dtype=jnp.int32)[:, None], (N_TOK, TOP_K))
    row_tok = jnp.zeros((_MOE_ROWS,), jnp.int32).at[dest.reshape(-1)].set(tok.reshape(-1))
    n_tiles = _MOE_ROWS // _MOE_TM
    tile_start = jnp.arange(n_tiles, dtype=jnp.int32) * _MOE_TM
    tile_expert = jnp.minimum(jnp.sum(tile_start[:, None] >= ends[None, :], axis=1), N_EXPERTS - 1).astype(jnp.int32)
    n_used = (ends[-1:] // _MOE_TM).astype(jnp.int32)
    ys = _moe_ffn_call(h, row_tok, w_gu, w_down, layer, tile_expert, n_used)
    return _combine_rows(ys, dest[:, 0], dest[:, 1], x, gate, route)


@functools.lru_cache(maxsize=None)
def _hgrn_tables():
    c = HG_CHUNK
    n_rows = c * (HG_LEVELS + 1) + HG_TOT_ROWS
    prefix = np.zeros((2, n_rows, c), np.float32)
    role = np.zeros((2, HG_LEVELS, c, 1), np.float32)
    pair = np.zeros((2, HG_LEVELS, c, c), np.float32)
    for d in range(2):
        for i in range(c):
            if d == 0:
                prefix[d, i, :i + 1] = 1.0
            else:
                prefix[d, i, i:] = 1.0
        for l in range(HG_LEVELS):
            m = 2 ** l
            for i in range(c):
                blk = i // (2 * m)
                upper = (i // m) % 2 == 1
                mid = blk * 2 * m + m
                row = c * (l + 1) + i
                if d == 0:
                    is_q = upper
                    lo, hi = (mid, i + 1) if upper else (i + 1, mid)
                else:
                    is_q = not upper
                    lo, hi = (mid, i) if upper else (i, mid)
                prefix[d, row, lo:hi] = 1.0
                role[d, l, i, 0] = 1.0 if is_q else 0.0
            for i in range(c):
                for j in range(c):
                    same = i // (2 * m) == j // (2 * m)
                    if same and role[d, l, i, 0] == 1.0 and role[d, l, j, 0] == 0.0:
                        pair[d, l, i, j] = 1.0
        prefix[d, c * (HG_LEVELS + 1):, :] = 1.0
    prefix = np.concatenate([prefix] * HG_SPLIT, axis=2)
    pair = np.concatenate([pair, pair], axis=3)
    return prefix, pair


def _pair_blockdiag(x2):
    zero = jnp.zeros((x2.shape[0], HG_DK), x2.dtype)
    return jnp.concatenate([jnp.concatenate([x2[:, :HG_DK], zero], axis=1),
                            jnp.concatenate([zero, x2[:, HG_DK:]], axis=1)], axis=0)


def _role_select(qh, k, level, backward):
    m = 2 ** level
    if m % 8 == 0:
        parts = []
        for r in range(0, HG_CHUNK, m):
            upper = (r // m) % 2 == 1
            parts.append((qh if upper != backward else k)[r:r + m])
        return jnp.concatenate(parts, axis=0)
    row = lax.broadcasted_iota(jnp.int32, qh.shape, 0)
    upper = (row // m) % 2 == 1
    return jnp.where(upper != backward, qh, k)


def _hgrn_dir_kernel(*refs, backward, hb, has_init, write_state):
    it = iter(refs)
    q_ref, f_ref, v_ref, la_ref, l1_ref, pm_ref, pair_ref = (next(it) for _ in range(7))
    s0_ref = next(it) if has_init else None
    if backward:
        of_ref, gt_ref, nw_ref = next(it), next(it), next(it)
    o_ref = next(it)
    st_ref = next(it) if write_state else None
    st_scr = next(it)

    c = HG_CHUNK
    n_chunk = HG_TB // c
    n_pair = hb // 2
    t = pl.program_id(2)

    @pl.when(t == 0)
    def _():
        for p in range(n_pair):
            if has_init:
                st_scr[p] = jnp.concatenate([s0_ref[2 * p].T, s0_ref[2 * p + 1].T], axis=1)
            else:
                st_scr[p] = jnp.zeros((HG_DV, HG_PAIR_W), F32)

    def pair_unit(p, ci):
        r0 = ci * c
        cols = slice(p * HG_PAIR_W, (p + 1) * HG_PAIR_W)
        qh = _silu(q_ref[pl.ds(r0, c), cols])
        z = f_ref[pl.ds(r0, c), cols]
        v = v_ref[pl.ds(r0, c), cols]
        la = la_ref[:, cols]
        b = l1_ref[:, cols] + (jnp.minimum(z, 0.0) - jnp.log(1.0 + jnp.exp(-jnp.abs(z))))
        g = jnp.maximum(la, b) + jnp.log(1.0 + jnp.exp(-jnp.abs(la - b)))
        k = 1.0 - jnp.exp(g)
        g1 = g.astype(BF16)
        g2 = (g - g1.astype(F32)).astype(BF16)
        sums = _dot(pm_ref[...], jnp.concatenate([g1, g2], axis=0))
        cum = sums[0:c]
        tot = sums[c * (HG_LEVELS + 1):c * (HG_LEVELS + 1) + 1]
        scores = jnp.zeros((c, HG_PAIR_W // 2), F32)
        for l in range(HG_LEVELS):
            e = jnp.exp(sums[c * (l + 1):c * (l + 2)])
            xk = (_role_select(qh, k, l, backward) * e).astype(BF16)
            scores = scores + pair_ref[l] * _dot_nt(xk, _pair_blockdiag(xk))
        vb = v.astype(BF16)
        qk = qh * k
        lane = lax.broadcasted_iota(jnp.int32, qk.shape, 1)
        self_score = jnp.where(lane < HG_DK, jnp.sum(qk[:, :HG_DK], axis=-1, keepdims=True),
                               jnp.sum(qk[:, HG_DK:], axis=-1, keepdims=True))
        o = _dot(scores.astype(BF16), _pair_blockdiag(vb)) + self_score * v
        st = st_scr[p]
        o = o + _dot_nt((qh * jnp.exp(cum)).astype(BF16), _pair_blockdiag(st.astype(BF16)))
        kt = (k * jnp.exp(tot - cum)).astype(BF16)
        v_rows = jnp.concatenate([vb[:, :HG_DV], vb[:, HG_DV:]], axis=0)
        st_scr[p] = st * jnp.exp(tot) + _dot_tn(v_rows, _pair_blockdiag(kt))
        if backward:
            o = o + of_ref[pl.ds(r0, c), cols]
            gate = _silu(gt_ref[pl.ds(r0, c), cols])
            for hh in range(2):
                hc = slice(hh * HG_DV, (hh + 1) * HG_DV)
                oh = o[:, hc]
                y = oh * lax.rsqrt(jnp.mean(oh * oh, axis=-1, keepdims=True) + EPS) * nw_ref[...]
                o_ref[pl.ds(r0, c), pl.ds(p * HG_PAIR_W + hh * HG_DV, HG_DV)] = (y * gate[:, hc]).astype(o_ref.dtype)
        else:
            o_ref[pl.ds(r0, c), cols] = o

    for i in range(n_chunk):
        ci = n_chunk - 1 - i if backward else i
        for p in range(n_pair):
            pair_unit(p, ci)

    if write_state:
        @pl.when(t == pl.num_programs(2) - 1)
        def _():
            for p in range(n_pair):
                st = st_scr[p]
                st_ref[2 * p] = st[:, :HG_DK].T
                st_ref[2 * p + 1] = st[:, HG_DK:].T


def _hgrn_scan(p, log_lb, log1m_lb, norm_w, rows0, seq, nb, hb, s0=None, write_state=False):
    w = hb * HG_DK
    ngrp = HG_HEADS // hb
    per = D_MODEL // w
    tb = HG_TB
    n_t = seq // tb
    rb0 = rows0 // tb
    prefix, pair = _hgrn_tables()
    outs = []
    o_fwd = None
    for d in range(2):
        def tblk(t, d=d):
            return n_t - 1 - t if d == 1 else t

        def col(group, d=d, tblk=tblk):
            return pl.BlockSpec((tb, w), lambda b, g, t: (rb0 + b * n_t + tblk(t), group * per + g))

        def row(d=d, tblk=tblk):
            return pl.BlockSpec((tb, w), lambda b, g, t: (b * n_t + tblk(t), g))

        in_specs = [col(0), col(1 + d), col(3),
                    pl.BlockSpec((None, 1, w), lambda b, g, t, d=d: (d, 0, g)),
                    pl.BlockSpec((None, 1, w), lambda b, g, t, d=d: (d, 0, g)),
                    pl.BlockSpec(prefix.shape[1:], lambda b, g, t: (0, 0)),
                    pl.BlockSpec(pair.shape[1:], lambda b, g, t: (0, 0, 0))]
        args = [p, p, p, log_lb, log1m_lb, jnp.asarray(prefix[d], BF16), jnp.asarray(pair[d])]
        if s0 is not None:
            in_specs.append(pl.BlockSpec((None, None, hb, HG_DK, HG_DV), lambda b, g, t, d=d: (b, d, g, 0, 0)))
            args.append(s0)
        if d == 1:
            in_specs += [row(), col(4), pl.BlockSpec((1, HG_DV), lambda b, g, t: (0, 0))]
            args += [o_fwd, p, norm_w.reshape(1, HG_DV)]
        out_specs = [row()]
        out_shape = [jax.ShapeDtypeStruct((nb * seq, D_MODEL), BF16 if d == 1 else F32)]
        if write_state:
            out_specs.append(pl.BlockSpec((None, hb, HG_DK, HG_DV), lambda b, g, t: (b, g, 0, 0)))
            out_shape.append(jax.ShapeDtypeStruct((nb, HG_HEADS, HG_DK, HG_DV), F32))
        vm = 2 * 7 * tb * w * 4 + 6 * hb * HG_DK * HG_DV * 4 + (16 << 20)
        res = pl.pallas_call(
            functools.partial(_hgrn_dir_kernel, backward=d == 1, hb=hb, has_init=s0 is not None,
                              write_state=write_state),
            grid=(nb, ngrp, n_t),
            in_specs=in_specs,
            out_specs=out_specs,
            out_shape=out_shape,
            scratch_shapes=[pltpu.VMEM((hb // 2, HG_DV, HG_PAIR_W), F32)],
            compiler_params=_cparams(("parallel", "parallel", "arbitrary"), vm),
            name="hgrn_bwd" if d == 1 else "hgrn_fwd",
        )(*args)
        if d == 0:
            o_fwd = res[0]
        outs.append(res)
    o = outs[1][0]
    if write_state:
        return o, jnp.stack([outs[0][1], outs[1][1]], axis=1)
    return (o,)


def _hgrn_mixer(hbf, w_in, w_out, layer, lb, norm_w, s0_lat, x, gate):
    p = _matmul(hbf, w_in, layer, _MM_TM, _MM_TN)
    log_lb = jnp.log(lb).reshape(2, 1, HG_FDIM)
    log1m_lb = jnp.log1p(-lb).reshape(2, 1, HG_FDIM)
    o_ctx, st = _hgrn_scan(p, log_lb, log1m_lb, norm_w, 0, SEQ, BATCH, 8, write_state=True)
    (o_lat,) = _hgrn_scan(p, log_lb, log1m_lb, norm_w, N_CTX, DEC_SEQ, DEC_BATCH, 8, s0=s0_lat)
    o = jnp.concatenate([o_ctx, o_lat], axis=0)
    return _matmul_residual(o, w_out, layer, x, gate, _MM_TM, _MM_TN), st


def _conv_kernel(b_ref, c_ref, x_ref, w_ref, o_ref):
    u = c_ref[...] * x_ref[...]
    seq = u.shape[0]
    row = lax.broadcasted_iota(jnp.int32, u.shape, 0)
    prev = jnp.where(row == 0, 0.0, pltpu.roll(u, 1, 0))
    nxt = jnp.where(row == seq - 1, 0.0, pltpu.roll(u, seq - 1, 0))
    w = w_ref[...]
    y = prev * w[0:1] + u * w[1:2] + nxt * w[2:3]
    o_ref[...] = (b_ref[...] * y).astype(o_ref.dtype)


def _conv_gate(p, conv_w, rows0, seq, nb):
    tc = 256
    per = D_MODEL // tc
    rb0 = rows0 // seq

    def col(group):
        return pl.BlockSpec((seq, tc), lambda b, j: (rb0 + b, group * per + j))

    return pl.pallas_call(
        _conv_kernel,
        grid=(nb, per),
        in_specs=[col(0), col(1), col(2), pl.BlockSpec((CONV_W, tc), lambda b, j: (0, j))],
        out_specs=pl.BlockSpec((seq, tc), lambda b, j: (b, j)),
        out_shape=jax.ShapeDtypeStruct((nb * seq, D_MODEL), BF16),
        compiler_params=_cparams(("parallel", "parallel"), 14 * seq * tc * 4),
        name="conv_gate",
    )(p, p, p, conv_w)


def _conv_mixer(hbf, w_in, w_out, layer, conv_w, x, gate):
    p = _matmul(hbf, w_in, layer, _MM_TM, _MM_TN)
    o = jnp.concatenate([_conv_gate(p, conv_w, 0, SEQ, BATCH),
                         _conv_gate(p, conv_w, N_CTX, DEC_SEQ, DEC_BATCH)], axis=0)
    return _matmul_residual(o, w_out, layer, x, gate, _MM_TM, _MM_TN)


def _head_norm(x, w):
    return x * lax.rsqrt(jnp.mean(x * x, axis=-1, keepdims=True) + EPS) * w


def _rope(x, cos, sin_signed):
    lane = lax.broadcasted_iota(jnp.int32, x.shape, 1)
    first = (lane % ROPE_AXIS_DIM) < (ROPE_AXIS_DIM // 2)
    rot = jnp.where(first, pltpu.roll(x, HEAD_DIM - ROPE_AXIS_DIM // 2, 1), pltpu.roll(x, ROPE_AXIS_DIM // 2, 1))
    return x * cos + rot * sin_signed


def _kvprep_kernel(*refs, rope):
    if rope:
        k_ref, v_ref, w_ref, cos_ref, sin_ref, kb_ref, vb_ref = refs
    else:
        k_ref, v_ref, w_ref, kn_ref, vf_ref, kb_ref, vb_ref = refs
        vf_ref[...] = v_ref[...]
    vb_ref[...] = v_ref[...].astype(BF16)
    for h in range(N_KV_HEADS):
        cols = slice(h * HEAD_DIM, (h + 1) * HEAD_DIM)
        kn = _head_norm(k_ref[:, cols], w_ref[...])
        if rope:
            kn = _rope(kn, cos_ref[...], sin_ref[...])
        else:
            kn_ref[:, cols] = kn
        kb_ref[:, cols] = kn.astype(BF16)


def _kv_prep(qkv, k_norm, rows0, nrows, rope_tabs):
    tm = 512
    kw = N_KV_HEADS * HEAD_DIM
    kcol = (N_HEADS * HEAD_DIM) // kw
    rb0 = rows0 // tm
    in_specs = [pl.BlockSpec((tm, kw), lambda i: (rb0 + i, kcol)),
                pl.BlockSpec((tm, kw), lambda i: (rb0 + i, kcol + 1)),
                pl.BlockSpec((1, HEAD_DIM), lambda i: (0, 0))]
    args = [qkv, qkv, k_norm.reshape(1, HEAD_DIM)]
    o_spec = pl.BlockSpec((tm, kw), lambda i: (i, 0))
    out_specs = [o_spec, o_spec]
    out_shape = [jax.ShapeDtypeStruct((nrows, kw), BF16)] * 2
    if rope_tabs is not None:
        per = DEC_SEQ // tm
        in_specs += [pl.BlockSpec((tm, HEAD_DIM), lambda i: (i % per, 0))] * 2
        args += list(rope_tabs)
    else:
        out_specs = [o_spec, o_spec] + out_specs
        out_shape = [jax.ShapeDtypeStruct((nrows, kw), F32)] * 2 + out_shape
    return pl.pallas_call(
        functools.partial(_kvprep_kernel, rope=rope_tabs is not None),
        grid=(nrows // tm,),
        in_specs=in_specs,
        out_specs=out_specs,
        out_shape=out_shape,
        compiler_params=_cparams(("parallel",), 16 * tm * kw * 4),
        name="kv_prep",
    )(*args)


def _attn_kernel(*refs, rope):
    if rope:
        q_ref, k_ref, v_ref, w_ref, cos_ref, sin_ref, o_ref = refs
    else:
        q_ref, k_ref, v_ref, w_ref, o_ref = refs
    kk = k_ref[...]
    vv = v_ref[...]
    scale = HEAD_DIM ** -0.5 * math.log2(math.e)
    for g in range(GQA_GROUP):
        cols = slice(g * HEAD_DIM, (g + 1) * HEAD_DIM)
        q = _head_norm(q_ref[:, cols], w_ref[...])
        if rope:
            q = _rope(q, cos_ref[...], sin_ref[...])
        s = _dot_nt((q * scale).astype(BF16), kk)
        p = jnp.exp2(s - jnp.max(s, axis=-1, keepdims=True))
        den = jnp.sum(p, axis=-1, keepdims=True)
        o = _dot(p.astype(BF16), vv) / den
        o_ref[:, cols] = o.astype(o_ref.dtype)


def _attention(qkv, kb, vb, q_norm, rows0, seq, nb, tk, rope_tabs):
    tq = 256
    gw = GQA_GROUP * HEAD_DIM
    nq = seq // tq
    rb0 = rows0 // tq
    in_specs = [pl.BlockSpec((tq, gw), lambda b, h, i: (rb0 + b * nq + i, h)),
                pl.BlockSpec((tk, HEAD_DIM), lambda b, h, i: (b, h)),
                pl.BlockSpec((tk, HEAD_DIM), lambda b, h, i: (b, h)),
                pl.BlockSpec((1, HEAD_DIM), lambda b, h, i: (0, 0))]
    args = [qkv, kb, vb, q_norm.reshape(1, HEAD_DIM)]
    if rope_tabs is not None:
        in_specs += [pl.BlockSpec((tq, HEAD_DIM), lambda b, h, i: (i, 0))] * 2
        args += list(rope_tabs)
    vm = 2 * (tq * gw * 4 + 2 * tk * HEAD_DIM * 2 + tq * gw * 2) + 4 * tq * tk * 4 + (4 << 20)
    return pl.pallas_call(
        functools.partial(_attn_kernel, rope=rope_tabs is not None),
        grid=(nb, N_KV_HEADS, nq),
        in_specs=in_specs,
        out_specs=pl.BlockSpec((tq, gw), lambda b, h, i: (b * nq + i, h)),
        out_shape=jax.ShapeDtypeStruct((nb * seq, D_MODEL), BF16),
        compiler_params=_cparams(("parallel", "parallel", "parallel"), vm),
        name="attention",
    )(*args)


def _rope_tables():
    n_rows = DEC_SEQ // GRID_W
    row = jnp.repeat(jnp.arange(n_rows), GRID_W).astype(F32)
    colp = jnp.tile(jnp.arange(GRID_W), n_rows).astype(F32)
    inv = ROPE_THETA ** (-jnp.arange(0, ROPE_AXIS_DIM, 2, dtype=F32) / ROPE_AXIS_DIM)
    ang_r = row[:, None] * inv
    ang_c = colp[:, None] * inv
    ang = jnp.concatenate([ang_r, ang_r, ang_c, ang_c], axis=-1)
    quarter = ROPE_AXIS_DIM // 2
    sign = jnp.where((jnp.arange(HEAD_DIM) % ROPE_AXIS_DIM) < quarter, -1.0, 1.0).astype(F32)
    return jnp.cos(ang), jnp.sin(ang) * sign


def _attn_mixer(hbf, w_qkv, w_out, layer, q_norm, k_norm, cache_k, cache_v, x, gate):
    qkv = _matmul(hbf, w_qkv, layer, _MM_TM, _MM_TN)
    kw = N_KV_HEADS * HEAD_DIM
    tabs = _rope_tables()
    kn_ctx, v_ctx, kb_ctx, vb_ctx = _kv_prep(qkv, k_norm, 0, N_CTX, None)
    kb_lat, vb_lat = _kv_prep(qkv, k_norm, N_CTX, N_LAT, tabs)
    o_ctx = _attention(qkv, kb_ctx, vb_ctx, q_norm, 0, SEQ, BATCH, SEQ, None)
    tk = PAST_LEN + DEC_SEQ
    keys = jnp.concatenate([cache_k.reshape(DEC_BATCH, PAST_LEN, kw).astype(BF16),
                            kb_lat.reshape(DEC_BATCH, DEC_SEQ, kw)], axis=1).reshape(DEC_BATCH * tk, kw)
    vals = jnp.concatenate([cache_v.reshape(DEC_BATCH, PAST_LEN, kw).astype(BF16),
                            vb_lat.reshape(DEC_BATCH, DEC_SEQ, kw)], axis=1).reshape(DEC_BATCH * tk, kw)
    o_lat = _attention(qkv, keys, vals, q_norm, N_CTX, DEC_SEQ, DEC_BATCH, tk, tabs)
    o = jnp.concatenate([o_ctx, o_lat], axis=0)
    new_k = kn_ctx.reshape(BATCH, SEQ, N_KV_HEADS, HEAD_DIM)
    new_v = v_ctx.reshape(BATCH, SEQ, N_KV_HEADS, HEAD_DIM)
    return _matmul_residual(o, w_out, layer, x, gate, _MM_TM, _MM_TN), new_k, new_v


def kernel(x_prompt, x_sample, state_hgrn, cache_k, cache_v, c, c_ctx, ada_w, ada_b, norm_w, final_norm_w,
           hgrn_w_in, hgrn_lb_logits, hgrn_norm_w, hgrn_w_out, conv_w_in, conv_w, conv_w_out,
           attn_w_qkv, attn_q_norm, attn_k_norm, attn_w_out, ffn_w_gate_up, ffn_w_down,
           moe_w_router, moe_w_gate_up, moe_w_down):
    x = jnp.concatenate([x_prompt.reshape(N_CTX, D_MODEL), x_sample.reshape(N_LAT, D_MODEL)], axis=0)
    conds = jnp.zeros((COND_PAD, D_MODEL), F32).at[0].set(c_ctx).at[1:N_COND].set(c)
    mod = _modulation(conds, ada_w, ada_b)
    mod = mod.reshape(DEPTH, COND_PAD, 6, 1, D_MODEL).transpose(0, 2, 1, 3, 4)

    probs = jax.nn.softmax(hgrn_lb_logits.astype(F32), axis=0)
    csum = jnp.cumsum(probs, axis=0)
    lower_bounds = csum - csum[:1]

    ffn_gu, ffn_down = ffn_w_gate_up.astype(BF16), ffn_w_down.astype(BF16)
    moe_gu, moe_down = moe_w_gate_up.astype(BF16), moe_w_down.astype(BF16)
    hgrn_w_in, hgrn_w_out = hgrn_w_in.astype(BF16), hgrn_w_out.astype(BF16)
    conv_w_in, conv_w_out = conv_w_in.astype(BF16), conv_w_out.astype(BF16)
    attn_w_qkv, attn_w_out = attn_w_qkv.astype(BF16), attn_w_out.astype(BF16)

    new_states, new_k, new_v = [], None, None
    for layer in range(DEPTH):
        kind = layer % N_MIXERS
        j = layer // N_MIXERS
        sh1, sc1, g1, sh2, sc2, g2 = (mod[layer, t] for t in range(6))
        hbf = _norm_mod(x, norm_w[layer, 0], sc1, sh1)
        if kind == 0:
            x, st = _hgrn_mixer(hbf, hgrn_w_in, hgrn_w_out, j, lower_bounds[j], hgrn_norm_w[j],
                                state_hgrn[:, j], x, g1)
            new_states.append(st)
        elif kind == 1:
            x = _conv_mixer(hbf, conv_w_in, conv_w_out, j, conv_w[j], x, g1)
        else:
            x, new_k, new_v = _attn_mixer(hbf, attn_w_qkv, attn_w_out, j, attn_q_norm[j], attn_k_norm[j],
                                          cache_k[:, j], cache_v[:, j], x, g1)
        f = layer // 2
        if layer % 2 == 0:
            hbf = _norm_mod(x, norm_w[layer, 1], sc2, sh2)
            x = _dense_ffn(hbf, ffn_gu, ffn_down, f, x, g2)
        else:
            x = _moe_ffn(x, norm_w[layer, 1], sc2, sh2, g2, moe_w_router[f], moe_gu, moe_down, f)
    y_prompt = _final_norm(x, final_norm_w, 0, N_CTX).reshape(BATCH, SEQ, D_MODEL)
    y_sample = _final_norm(x, final_norm_w, N_CTX, N_LAT).reshape(DEC_BATCH, DEC_SEQ, D_MODEL)
    new_state_hgrn = jnp.stack(new_states, axis=1)
    return (y_prompt, y_sample, new_state_hgrn, new_k[:, None], new_v[:, None])
```

```python
import functools
import math

import numpy as np
import jax
import jax.numpy as jnp
from jax import lax
from jax.experimental import pallas as pl
from jax.experimental.pallas import tpu as pltpu

F32 = jnp.float32
BF16 = jnp.bfloat16

D_MODEL = 2048
BATCH = 32
SEQ = 256
DEPTH = 4
DEC_BATCH = 8
DEC_SEQ = 2048
PAST_LEN = 512
GRID_W = 64
N_MIXERS = 3
EPS = 1e-6
HG_HEADS = 16
HG_DK = 128
HG_DV = 128
HG_FDIM = HG_HEADS * HG_DK
CONV_W = 3
HEAD_DIM = 128
N_HEADS = 16
N_KV_HEADS = 4
GQA_GROUP = 4
ROPE_THETA = 10000.0
ROPE_AXIS_DIM = HEAD_DIM // 2
D_FF = 5632
N_EXPERTS = 8
TOP_K = 2

N_CTX = BATCH * SEQ
N_LAT = DEC_BATCH * DEC_SEQ
N_TOK = N_CTX + N_LAT
N_COND = 1 + DEC_BATCH
COND_PAD = 16

V7X_LANES = 128
LANE_CHUNKS = D_MODEL // V7X_LANES
V7X_VMEM_BYTES = 64 * 1024 * 1024
VMEM_CAP = 56 * 1024 * 1024

HG_CHUNK = 64
HG_LEVELS = int(math.log2(HG_CHUNK))
HG_TOT_ROWS = 16
HG_SPLIT = 2
HG_PAIR_W = 2 * HG_DK
HG_TB = 256


def _cparams(sem, vmem_bytes):
    return pltpu.CompilerParams(dimension_semantics=sem,
                                vmem_limit_bytes=int(min(VMEM_CAP, max(vmem_bytes, 16 * 1024 * 1024))))


def _cond_of_tile(i, tm):
    r = i * tm
    return jnp.where(r < N_CTX, 0, 1 + (r - N_CTX) // DEC_SEQ)


def _silu(x):
    return x * jax.nn.sigmoid(x)


def _dot(a, b):
    return jnp.dot(a, b, preferred_element_type=F32)


def _dot_nt(a, b):
    return lax.dot_general(a, b, (((1,), (1,)), ((), ())), preferred_element_type=F32)


def _dot_tn(a, b):
    return lax.dot_general(a, b, (((0,), (0,)), ((), ())), preferred_element_type=F32)


def _mod_kernel(c_ref, w_ref, b_ref, o_ref):
    a = _silu(c_ref[...]).astype(BF16)
    o_ref[...] = _dot(a, w_ref[...].astype(BF16)) + b_ref[...]


def _modulation(conds, ada_w, ada_b):
    tn = 1024
    n_out = 6 * D_MODEL
    return pl.pallas_call(
        _mod_kernel,
        grid=(DEPTH, n_out // tn),
        in_specs=[pl.BlockSpec((COND_PAD, D_MODEL), lambda l, j: (0, 0)),
                  pl.BlockSpec((None, D_MODEL, tn), lambda l, j: (l, 0, j)),
                  pl.BlockSpec((None, 1, tn), lambda l, j: (l, 0, j))],
        out_specs=pl.BlockSpec((None, COND_PAD, tn), lambda l, j: (l, 0, j)),
        out_shape=jax.ShapeDtypeStruct((DEPTH, COND_PAD, n_out), F32),
        compiler_params=_cparams(("parallel", "parallel"), 3 * D_MODEL * tn * 4 + (4 << 20)),
        name="modulation",
    )(conds, ada_w, ada_b.reshape(DEPTH, 1, n_out))


def _norm_body(x_ref, w_ref, sc_ref, sh_ref):
    x = x_ref[...]
    y = x * lax.rsqrt(jnp.mean(x * x, axis=-1, keepdims=True) + EPS) * w_ref[...]
    return y * (1.0 + sc_ref[...]) + sh_ref[...]


def _norm_mod_kernel(x_ref, w_ref, sc_ref, sh_ref, o_ref):
    o_ref[...] = _norm_body(x_ref, w_ref, sc_ref, sh_ref).astype(o_ref.dtype)


def _final_norm_kernel(x_ref, w_ref, o_ref):
    x = x_ref[...]
    o_ref[...] = x * lax.rsqrt(jnp.mean(x * x, axis=-1, keepdims=True) + EPS) * w_ref[...]


def _norm_router_kernel(x_ref, w_ref, sc_ref, sh_ref, rh_ref, rl_ref, h_ref, r_ref):
    h = _norm_body(x_ref, w_ref, sc_ref, sh_ref)
    tm = h.shape[0]
    for cc in range(LANE_CHUNKS):
        h_ref[pl.ds(cc, tm, stride=LANE_CHUNKS), :] = h[:, cc * V7X_LANES:(cc + 1) * V7X_LANES]
    hh = h.astype(BF16)
    hl = (h - hh.astype(F32)).astype(BF16)
    logits = _dot(hh, rh_ref[...]) + (_dot(hh, rl_ref[...]) + _dot(hl, rh_ref[...]))
    lane = lax.broadcasted_iota(jnp.int32, logits.shape, 1)
    neg = jnp.float32(-jnp.inf)
    logits = jnp.where(lane < N_EXPERTS, logits, neg)
    v1 = jnp.max(logits, axis=-1, keepdims=True)
    i1 = jnp.min(jnp.where(logits == v1, lane, V7X_LANES), axis=-1, keepdims=True)
    rest = jnp.where(lane == i1, neg, logits)
    v2 = jnp.max(rest, axis=-1, keepdims=True)
    i2 = jnp.min(jnp.where(rest == v2, lane, V7X_LANES), axis=-1, keepdims=True)
    e = jnp.exp(v2 - v1)
    g1 = 1.0 / (1.0 + e)
    g2 = e * g1
    out = jnp.where(lane == 0, i1.astype(F32), 0.0)
    out = jnp.where(lane == 1, i2.astype(F32), out)
    out = jnp.where(lane == 2, g1, out)
    out = jnp.where(lane == 3, g2, out)
    r_ref[...] = out


_NORM_TM = 512


def _row_specs(tm):
    x_spec = pl.BlockSpec((tm, D_MODEL), lambda i: (i, 0))
    w_spec = pl.BlockSpec((1, D_MODEL), lambda i: (0, 0))
    c_spec = pl.BlockSpec((None, 1, D_MODEL), lambda i: (_cond_of_tile(i, tm), 0, 0))
    return x_spec, w_spec, c_spec


def _norm_mod(x, w, sc, sh, out_dtype=BF16):
    tm = _NORM_TM
    x_spec, w_spec, c_spec = _row_specs(tm)
    return pl.pallas_call(
        _norm_mod_kernel,
        grid=(N_TOK // tm,),
        in_specs=[x_spec, w_spec, c_spec, c_spec],
        out_specs=x_spec,
        out_shape=jax.ShapeDtypeStruct((N_TOK, D_MODEL), out_dtype),
        compiler_params=_cparams(("parallel",), 6 * tm * D_MODEL * 4),
        name="norm_mod",
    )(x, w.reshape(1, D_MODEL), sc, sh)


def _final_norm(x, w, rows0, nrows):
    tm = _NORM_TM
    rb0 = rows0 // tm
    return pl.pallas_call(
        _final_norm_kernel,
        grid=(nrows // tm,),
        in_specs=[pl.BlockSpec((tm, D_MODEL), lambda i: (rb0 + i, 0)),
                  pl.BlockSpec((1, D_MODEL), lambda i: (0, 0))],
        out_specs=pl.BlockSpec((tm, D_MODEL), lambda i: (i, 0)),
        out_shape=jax.ShapeDtypeStruct((nrows, D_MODEL), F32),
        compiler_params=_cparams(("parallel",), 6 * tm * D_MODEL * 4),
        name="final_norm",
    )(x, w.reshape(1, D_MODEL))


def _norm_router(x, w, sc, sh, w_router):
    tm = _NORM_TM
    x_spec, w_spec, c_spec = _row_specs(tm)
    wr = jnp.zeros((D_MODEL, V7X_LANES), F32).at[:, :N_EXPERTS].set(w_router)
    wr_hi = wr.astype(BF16)
    wr_lo = (wr - wr_hi.astype(F32)).astype(BF16)
    r_spec = pl.BlockSpec((D_MODEL, V7X_LANES), lambda i: (0, 0))
    return pl.pallas_call(
        _norm_router_kernel,
        grid=(N_TOK // tm,),
        in_specs=[x_spec, w_spec, c_spec, c_spec, r_spec, r_spec],
        out_specs=[pl.BlockSpec((tm * LANE_CHUNKS, V7X_LANES), lambda i: (i, 0)),
                   pl.BlockSpec((tm, V7X_LANES), lambda i: (i, 0))],
        out_shape=[jax.ShapeDtypeStruct((N_TOK * LANE_CHUNKS, V7X_LANES), F32),
                   jax.ShapeDtypeStruct((N_TOK, V7X_LANES), F32)],
        compiler_params=_cparams(("parallel",), 8 * tm * D_MODEL * 4),
        name="norm_router",
    )(x, w.reshape(1, D_MODEL), sc, sh, wr_hi, wr_lo)


_MM_TM = 1024
_MM_TN = 1024


def _mm_kernel(a_ref, w_ref, o_ref):
    o_ref[...] = _dot(a_ref[...], w_ref[...]).astype(o_ref.dtype)


def _mm_res_kernel(a_ref, w_ref, x_ref, g_ref, o_ref):
    o_ref[...] = x_ref[...] + g_ref[...] * _dot(a_ref[...], w_ref[...])


def _matmul(a, w_all, layer, tm, tn, out_dtype=F32):
    m, k = a.shape
    n = w_all.shape[2]
    wb = w_all.dtype.itemsize
    vm = 2 * (tm * k * 2 + k * tn * wb + tm * tn * 4) + tm * tn * 4 + k * tn * 2
    return pl.pallas_call(
        _mm_kernel,
        grid=(m // tm, n // tn),
        in_specs=[pl.BlockSpec((tm, k), lambda i, j: (i, 0)),
                  pl.BlockSpec((None, k, tn), lambda i, j: (layer, 0, j))],
        out_specs=pl.BlockSpec((tm, tn), lambda i, j: (i, j)),
        out_shape=jax.ShapeDtypeStruct((m, n), out_dtype),
        compiler_params=_cparams(("parallel", "parallel"), vm + (4 << 20)),
        name="matmul",
    )(a, w_all)


def _matmul_residual(a, w_all, layer, x, gate, tm, tn):
    m, k = a.shape
    n = w_all.shape[2]
    wb = w_all.dtype.itemsize
    vm = 2 * (tm * k * 2 + k * tn * wb + 2 * tm * tn * 4) + tm * tn * 4 + k * tn * 2
    return pl.pallas_call(
        _mm_res_kernel,
        grid=(m // tm, n // tn),
        in_specs=[pl.BlockSpec((tm, k), lambda i, j: (i, 0)),
                  pl.BlockSpec((None, k, tn), lambda i, j: (layer, 0, j)),
                  pl.BlockSpec((tm, tn), lambda i, j: (i, j)),
                  pl.BlockSpec((None, 1, tn), lambda i, j: (_cond_of_tile(i, tm), 0, j))],
        out_specs=pl.BlockSpec((tm, tn), lambda i, j: (i, j)),
        out_shape=jax.ShapeDtypeStruct((m, n), F32),
        compiler_params=_cparams(("parallel", "parallel"), vm + (4 << 20)),
        name="matmul_residual",
    )(a, w_all, x, gate)


_FFN_TF = 512


def _ffn_partial(x_ref, wa_ref, wg_ref, wd_ref):
    x = x_ref[...]
    a = _dot(x, wa_ref[...])
    g = _dot(x, wg_ref[...])
    hid = (_silu(a) * g).astype(BF16)
    return _dot(hid, wd_ref[...])


def _ffn_moe_kernel(te_ref, nu_ref, rt_ref, h_hbm, wa_ref, wg_ref, wd_ref, o_ref, stage, xb, sem):
    i = pl.program_id(0)
    f = pl.program_id(1)
    nt = pl.num_programs(0)
    nf = pl.num_programs(1)
    tm = o_ref.shape[0]
    lc = LANE_CHUNKS
    rows = stage.shape[1] // lc
    per = rows // (D_FF // _FFN_TF)
    slot = i % 2

    def issue(tile, to_slot, step):
        for k in range(per):
            r = step * per + k
            tok = rt_ref[tile * tm + jnp.minimum(r, tm - 1)]
            src = h_hbm.at[pl.ds(pl.multiple_of(tok * lc, lc), lc), :]
            dst = stage.at[to_slot, pl.ds(pl.multiple_of(r * lc, lc), lc), :]
            pltpu.make_async_copy(src, dst, sem.at[to_slot]).start()

    def wait(on_slot):
        pltpu.make_async_copy(h_hbm.at[pl.ds(0, rows * lc), :], stage.at[on_slot], sem.at[on_slot]).wait()

    @pl.when((i == 0) & (f == 0))
    def _():
        def body(step, carry):
            issue(0, 0, step)
            return carry
        lax.fori_loop(0, nf, body, 0)

    @pl.when(f == 0)
    def _():
        wait(slot)
        for cc in range(lc):
            xb[:, cc * V7X_LANES:(cc + 1) * V7X_LANES] = stage[slot, pl.ds(cc, tm, stride=lc), :].astype(BF16)
        o_ref[...] = jnp.zeros(o_ref.shape, o_ref.dtype)

    nxt = jnp.minimum(i + 1, nt - 1)

    @pl.when(i < nu_ref[0])
    def _():
        issue(nxt, 1 - slot, f)
        o_ref[...] += _ffn_partial(xb, wa_ref, wg_ref, wd_ref)

    @pl.when(i >= nu_ref[0])
    def _():
        issue(nxt, 1 - slot, f)

    @pl.when((i == nt - 1) & (f == nf - 1))
    def _():
        wait(1 - slot)


def _moe_ffn_call(h, row_tok, w_gu, w_down, layer, tile_expert, n_used):
    tm = _MOE_TM
    tf = _FFN_TF
    nf = D_FF // tf
    per = -(-tm // (nf * 8)) * 8
    stage_rows = per * nf
    vm = (2 * stage_rows * D_MODEL * 4 + tm * D_MODEL * 2 + 2 * (3 * D_MODEL * tf * 2 + tm * D_MODEL * 4)
          + tm * D_MODEL * 4 + 3 * tm * tf * 4)
    return pl.pallas_call(
        _ffn_moe_kernel,
        grid_spec=pltpu.PrefetchScalarGridSpec(
            num_scalar_prefetch=3,
            grid=(_MOE_ROWS // tm, nf),
            in_specs=[pl.BlockSpec(memory_space=pl.ANY),
                      pl.BlockSpec((None, None, D_MODEL, tf), lambda i, f, te, nu, rt: (layer, te[i], 0, f)),
                      pl.BlockSpec((None, None, D_MODEL, tf), lambda i, f, te, nu, rt: (layer, te[i], 0, f + nf)),
                      pl.BlockSpec((None, None, tf, D_MODEL), lambda i, f, te, nu, rt: (layer, te[i], f, 0))],
            out_specs=pl.BlockSpec((tm, D_MODEL), lambda i, f, te, nu, rt: (i, 0)),
            scratch_shapes=[pltpu.VMEM((2, stage_rows * LANE_CHUNKS, V7X_LANES), F32),
                            pltpu.VMEM((tm, D_MODEL), BF16),
                            pltpu.SemaphoreType.DMA((2,))]),
        out_shape=jax.ShapeDtypeStruct((_MOE_ROWS, D_MODEL), F32),
        compiler_params=_cparams(("arbitrary", "arbitrary"), vm + (4 << 20)),
        name="moe_swiglu",
    )(tile_expert, n_used, row_tok, h, w_gu, w_gu, w_down)


def _ffn_res_kernel(x_ref, wa_ref, wg_ref, wd_ref, r_ref, g_ref, o_ref):
    @pl.when(pl.program_id(1) == 0)
    def _():
        o_ref[...] = r_ref[...]

    o_ref[...] += g_ref[...] * _ffn_partial(x_ref, wa_ref, wg_ref, wd_ref)


def _dense_ffn(hb, w_gu, w_down, layer, x, gate):
    tm = 512
    tf = _FFN_TF
    nf = D_FF // tf
    vm = 2 * (tm * D_MODEL * 2 + 3 * D_MODEL * tf * 2 + 2 * tm * D_MODEL * 4) + tm * D_MODEL * 4 + 3 * tm * tf * 4
    return pl.pallas_call(
        _ffn_res_kernel,
        grid=(N_TOK // tm, nf),
        in_specs=[pl.BlockSpec((tm, D_MODEL), lambda i, f: (i, 0)),
                  pl.BlockSpec((None, D_MODEL, tf), lambda i, f: (layer, 0, f)),
                  pl.BlockSpec((None, D_MODEL, tf), lambda i, f: (layer, 0, f + nf)),
                  pl.BlockSpec((None, tf, D_MODEL), lambda i, f: (layer, f, 0)),
                  pl.BlockSpec((tm, D_MODEL), lambda i, f: (i, 0)),
                  pl.BlockSpec((None, 1, D_MODEL), lambda i, f: (_cond_of_tile(i, tm), 0, 0))],
        out_specs=pl.BlockSpec((tm, D_MODEL), lambda i, f: (i, 0)),
        out_shape=jax.ShapeDtypeStruct((N_TOK, D_MODEL), F32),
        compiler_params=_cparams(("parallel", "arbitrary"), vm + (4 << 20)),
        name="swiglu",
    )(hb, w_gu, w_gu, w_down, x, gate)


_MOE_TM = 512
_MOE_ROWS = TOP_K * N_TOK + N_EXPERTS * _MOE_TM
_GATHER_TM = 256
_DMA_ISSUE_UNROLL = 8


def _combine_kernel(d0_ref, d1_ref, y_hbm, x_ref, gt_ref, w_ref, o_ref, buf, sem):
    i = pl.program_id(0)
    tm = o_ref.shape[0]

    def issue(tile, slot):
        def body(r, carry):
            t = tile * tm + r
            pltpu.make_async_copy(y_hbm.at[pl.ds(d0_ref[t], 1), :], buf.at[slot, 0, pl.ds(r, 1), :],
                                  sem.at[slot]).start()
            pltpu.make_async_copy(y_hbm.at[pl.ds(d1_ref[t], 1), :], buf.at[slot, 1, pl.ds(r, 1), :],
                                  sem.at[slot]).start()
            return carry
        lax.fori_loop(0, tm, body, 0, unroll=_DMA_ISSUE_UNROLL)

    @pl.when(i == 0)
    def _():
        issue(0, 0)

    @pl.when(i + 1 < pl.num_programs(0))
    def _():
        issue(i + 1, (i + 1) % 2)

    slot = i % 2
    pltpu.make_async_copy(y_hbm.at[pl.ds(0, tm), :], buf.at[slot, 0], sem.at[slot]).wait()
    pltpu.make_async_copy(y_hbm.at[pl.ds(0, tm), :], buf.at[slot, 1], sem.at[slot]).wait()
    w = w_ref[...]
    y = w[:, 2:3] * buf[slot, 0] + w[:, 3:4] * buf[slot, 1]
    o_ref[...] = x_ref[...] + gt_ref[...] * y


def _combine_rows(y, d0, d1, x, gate, route):
    tm = _GATHER_TM
    return pl.pallas_call(
        _combine_kernel,
        grid_spec=pltpu.PrefetchScalarGridSpec(
            num_scalar_prefetch=2,
            grid=(N_TOK // tm,),
            in_specs=[pl.BlockSpec(memory_space=pl.ANY),
                      pl.BlockSpec((tm, D_MODEL), lambda i, a, b: (i, 0)),
                      pl.BlockSpec((None, 1, D_MODEL), lambda i, a, b: (_cond_of_tile(i, tm), 0, 0)),
                      pl.BlockSpec((tm, V7X_LANES), lambda i, a, b: (i, 0))],
            out_specs=pl.BlockSpec((tm, D_MODEL), lambda i, a, b: (i, 0)),
            scratch_shapes=[pltpu.VMEM((2, 2, tm, D_MODEL), F32), pltpu.SemaphoreType.DMA((2,))]),
        out_shape=jax.ShapeDtypeStruct((N_TOK, D_MODEL), F32),
        compiler_params=_cparams(("arbitrary",), 10 * tm * D_MODEL * 4),
        name="moe_combine",
    )(d0, d1, y, x, gate, route)


def _moe_ffn(x, norm_w, sc, sh, gate, w_router, w_gu, w_down, layer):
    h, route = _norm_router(x, norm_w, sc, sh, w_router)
    top_i = route[:, :TOP_K].astype(jnp.int32)
    member = jnp.sum(top_i[:, :, None] == jnp.arange(N_EXPERTS, dtype=jnp.int32), axis=1, dtype=jnp.int32)
    before = jnp.cumsum(member, axis=0) - member
    counts = before[-1] + member[-1]
    padded = ((counts + _MOE_TM - 1) // _MOE_TM) * _MOE_TM
    ends = jnp.cumsum(padded)
    starts = ends - padded
    dest = jnp.take(starts, top_i) + jnp.take_along_axis(before, top_i, axis=1)
    tok = jnp.broadcast_to(jnp.arange(N_TOK, dtype=jnp.int32)[:, None], (N_TOK, TOP_K))
    row_tok = jnp.zeros((_MOE_ROWS,), jnp.int32).at[dest.reshape(-1)].set(tok.reshape(-1))
    n_tiles = _MOE_ROWS // _MOE_TM
    tile_start = jnp.arange(n_tiles, dtype=jnp.int32) * _MOE_TM
    tile_expert = jnp.minimum(jnp.sum(tile_start[:, None] >= ends[None, :], axis=1), N_EXPERTS - 1).astype(jnp.int32)
    n_used = (ends[-1:] // _MOE_TM).astype(jnp.int32)
    ys = _moe_ffn_call(h, row_tok, w_gu, w_down, layer, tile_expert, n_used)
    return _combine_rows(ys, dest[:, 0], dest[:, 1], x, gate, route)


@functools.lru_cache(maxsize=None)
def _hgrn_tables():
    c = HG_CHUNK
    n_rows = c * (HG_LEVELS + 1) + HG_TOT_ROWS
    prefix = np.zeros((2, n_rows, c), np.float32)
    role = np.zeros((2, HG_LEVELS, c, 1), np.float32)
    pair = np.zeros((2, HG_LEVELS, c, c), np.float32)
    for d in range(2):
        for i in range(c):
            if d == 0:
                prefix[d, i, :i + 1] = 1.0
            else:
                prefix[d, i, i:] = 1.0
        for l in range(HG_LEVELS):
            m = 2 ** l
            for i in range(c):
                blk = i // (2 * m)
                upper = (i // m) % 2 == 1
                mid = blk * 2 * m + m
                row = c * (l + 1) + i
                if d == 0:
                    is_q = upper
                    lo, hi = (mid, i + 1) if upper else (i + 1, mid)
                else:
                    is_q = not upper
                    lo, hi = (mid, i) if upper else (i, mid)
                prefix[d, row, lo:hi] = 1.0
                role[d, l, i, 0] = 1.0 if is_q else 0.0
            for i in range(c):
                for j in range(c):
                    same = i // (2 * m) == j // (2 * m)
                    if same and role[d, l, i, 0] == 1.0 and role[d, l, j, 0] == 0.0:
                        pair[d, l, i, j] = 1.0
        prefix[d, c * (HG_LEVELS + 1):, :] = 1.0
    prefix = np.concatenate([prefix] * HG_SPLIT, axis=2)
    pair = np.concatenate([pair, pair], axis=3)
    return prefix, pair


def _pair_blockdiag(x2):
    zero = jnp.zeros((x2.shape[0], HG_DK), x2.dtype)
    return jnp.concatenate([jnp.concatenate([x2[:, :HG_DK], zero], axis=1),
                            jnp.concatenate([zero, x2[:, HG_DK:]], axis=1)], axis=0)


def _role_select(qh, k, level, backward):
    m = 2 ** level
    if m % 8 == 0:
        parts = []
        for r in range(0, HG_CHUNK, m):
            upper = (r // m) % 2 == 1
            parts.append((qh if upper != backward else k)[r:r + m])
        return jnp.concatenate(parts, axis=0)
    row = lax.broadcasted_iota(jnp.int32, qh.shape, 0)
    upper = (row // m) % 2 == 1
    return jnp.where(upper != backward, qh, k)


def _hgrn_dir_kernel(*refs, backward, hb, has_init, write_state):
    it = iter(refs)
    q_ref, f_ref, v_ref, la_ref, l1_ref, pm_ref, pair_ref = (next(it) for _ in range(7))
    s0_ref = next(it) if has_init else None
    if backward:
        of_ref, gt_ref, nw_ref = next(it), next(it), next(it)
    o_ref = next(it)
    st_ref = next(it) if write_state else None
    st_scr = next(it)

    c = HG_CHUNK
    n_chunk = HG_TB // c
    n_pair = hb // 2
    t = pl.program_id(2)

    @pl.when(t == 0)
    def _():
        for p in range(n_pair):
            if has_init:
                st_scr[p] = jnp.concatenate([s0_ref[2 * p].T, s0_ref[2 * p + 1].T], axis=1)
            else:
                st_scr[p] = jnp.zeros((HG_DV, HG_PAIR_W), F32)

    def pair_unit(p, ci):
        r0 = ci * c
        cols = slice(p * HG_PAIR_W, (p + 1) * HG_PAIR_W)
        qh = _silu(q_ref[pl.ds(r0, c), cols])
        z = f_ref[pl.ds(r0, c), cols]
        v = v_ref[pl.ds(r0, c), cols]
        la = la_ref[:, cols]
        b = l1_ref[:, cols] + (jnp.minimum(z, 0.0) - jnp.log(1.0 + jnp.exp(-jnp.abs(z))))
        g = jnp.maximum(la, b) + jnp.log(1.0 + jnp.exp(-jnp.abs(la - b)))
        g = g * math.log2(math.e)
        k = 1.0 - jnp.exp2(g)
        g1 = g.astype(BF16)
        g2 = (g - g1.astype(F32)).astype(BF16)
        sums = _dot(pm_ref[...], jnp.concatenate([g1, g2], axis=0))
        cum = sums[0:c]
        tot = sums[c * (HG_LEVELS + 1):c * (HG_LEVELS + 1) + 1]
        scores = jnp.zeros((c, HG_PAIR_W // 2), F32)
        for l in range(HG_LEVELS):
            e = jnp.exp2(sums[c * (l + 1):c * (l + 2)])
            xk = (_role_select(qh, k, l, backward) * e).astype(BF16)
            scores = scores + pair_ref[l] * _dot_nt(xk, _pair_blockdiag(xk))
        vb = v.astype(BF16)
        qk = qh * k
        lane = lax.broadcasted_iota(jnp.int32, qk.shape, 1)
        self_score = jnp.where(lane < HG_DK, jnp.sum(qk[:, :HG_DK], axis=-1, keepdims=True),
                               jnp.sum(qk[:, HG_DK:], axis=-1, keepdims=True))
        o = _dot(scores.astype(BF16), _pair_blockdiag(vb)) + self_score * v
        st = st_scr[p]
        o = o + _dot_nt((qh * jnp.exp2(cum)).astype(BF16), _pair_blockdiag(st.astype(BF16)))
        kt = (k * jnp.exp2(tot - cum)).astype(BF16)
        v_rows = jnp.concatenate([vb[:, :HG_DV], vb[:, HG_DV:]], axis=0)
        st_scr[p] = st * jnp.exp2(tot) + _dot_tn(v_rows, _pair_blockdiag(kt))
        if backward:
            o = o + of_ref[pl.ds(r0, c), cols]
            gate = _silu(gt_ref[pl.ds(r0, c), cols])
            for hh in range(2):
                hc = slice(hh * HG_DV, (hh + 1) * HG_DV)
                oh = o[:, hc]
                y = oh * lax.rsqrt(jnp.mean(oh * oh, axis=-1, keepdims=True) + EPS) * nw_ref[...]
                o_ref[pl.ds(r0, c), pl.ds(p * HG_PAIR_W + hh * HG_DV, HG_DV)] = (y * gate[:, hc]).astype(o_ref.dtype)
        else:
            o_ref[pl.ds(r0, c), cols] = o

    for i in range(n_chunk):
        ci = n_chunk - 1 - i if backward else i
        for p in range(n_pair):
            pair_unit(p, ci)

    if write_state:
        @pl.when(t == pl.num_programs(2) - 1)
        def _():
            for p in range(n_pair):
                st = st_scr[p]
                st_ref[2 * p] = st[:, :HG_DK].T
                st_ref[2 * p + 1] = st[:, HG_DK:].T


def _hgrn_scan(p, log_lb, log1m_lb, norm_w, rows0, seq, nb, hb, s0=None, write_state=False):
    w = hb * HG_DK
    ngrp = HG_HEADS // hb
    per = D_MODEL // w
    tb = HG_TB
    n_t = seq // tb
    rb0 = rows0 // tb
    prefix, pair = _hgrn_tables()
    outs = []
    o_fwd = None
    for d in range(2):
        def tblk(t, d=d):
            return n_t - 1 - t if d == 1 else t

        def col(group, d=d, tblk=tblk):
            return pl.BlockSpec((tb, w), lambda b, g, t: (rb0 + b * n_t + tblk(t), group * per + g))

        def row(d=d, tblk=tblk):
            return pl.BlockSpec((tb, w), lambda b, g, t: (b * n_t + tblk(t), g))

        in_specs = [col(0), col(1 + d), col(3),
                    pl.BlockSpec((None, 1, w), lambda b, g, t, d=d: (d, 0, g)),
                    pl.BlockSpec((None, 1, w), lambda b, g, t, d=d: (d, 0, g)),
                    pl.BlockSpec(prefix.shape[1:], lambda b, g, t: (0, 0)),
                    pl.BlockSpec(pair.shape[1:], lambda b, g, t: (0, 0, 0))]
        args = [p, p, p, log_lb, log1m_lb, jnp.asarray(prefix[d], BF16), jnp.asarray(pair[d])]
        if s0 is not None:
            in_specs.append(pl.BlockSpec((None, None, hb, HG_DK, HG_DV), lambda b, g, t, d=d: (b, d, g, 0, 0)))
            args.append(s0)
        if d == 1:
            in_specs += [row(), col(4), pl.BlockSpec((1, HG_DV), lambda b, g, t: (0, 0))]
            args += [o_fwd, p, norm_w.reshape(1, HG_DV)]
        out_specs = [row()]
        out_shape = [jax.ShapeDtypeStruct((nb * seq, D_MODEL), BF16 if d == 1 else F32)]
        if write_state:
            out_specs.append(pl.BlockSpec((None, hb, HG_DK, HG_DV), lambda b, g, t: (b, g, 0, 0)))
            out_shape.append(jax.ShapeDtypeStruct((nb, HG_HEADS, HG_DK, HG_DV), F32))
        vm = 2 * 7 * tb * w * 4 + 6 * hb * HG_DK * HG_DV * 4 + (16 << 20)
        res = pl.pallas_call(
            functools.partial(_hgrn_dir_kernel, backward=d == 1, hb=hb, has_init=s0 is not None,
                              write_state=write_state),
            grid=(nb, ngrp, n_t),
            in_specs=in_specs,
            out_specs=out_specs,
            out_shape=out_shape,
            scratch_shapes=[pltpu.VMEM((hb // 2, HG_DV, HG_PAIR_W), F32)],
            compiler_params=_cparams(("parallel", "parallel", "arbitrary"), vm),
            name="hgrn_bwd" if d == 1 else "hgrn_fwd",
        )(*args)
        if d == 0:
            o_fwd = res[0]
        outs.append(res)
    o = outs[1][0]
    if write_state:
        return o, jnp.stack([outs[0][1], outs[1][1]], axis=1)
    return (o,)


def _hgrn_mixer(hbf, w_in, w_out, layer, lb, norm_w, s0_lat, x, gate):
    p = _matmul(hbf, w_in, layer, _MM_TM, _MM_TN)
    log_lb = jnp.log(lb).reshape(2, 1, HG_FDIM)
    log1m_lb = jnp.log1p(-lb).reshape(2, 1, HG_FDIM)
    o_ctx, st = _hgrn_scan(p, log_lb, log1m_lb, norm_w, 0, SEQ, BATCH, 16, write_state=True)
    (o_lat,) = _hgrn_scan(p, log_lb, log1m_lb, norm_w, N_CTX, DEC_SEQ, DEC_BATCH, 16, s0=s0_lat)
    o = jnp.concatenate([o_ctx, o_lat], axis=0)
    return _matmul_residual(o, w_out, layer, x, gate, _MM_TM, _MM_TN), st


_CONV_TM = DEC_SEQ


def _conv_mm_kernel(a_ref, wb_ref, wc_ref, wx_ref, cw_ref, o_ref):
    a = a_ref[...]
    b_gate = _dot(a, wb_ref[...])
    u = _dot(a, wc_ref[...]) * _dot(a, wx_ref[...])
    tm = u.shape[0]
    seq = jnp.where(pl.program_id(0) * tm < N_CTX, SEQ, DEC_SEQ)
    pos = lax.broadcasted_iota(jnp.int32, u.shape, 0) & (seq - 1)
    prev = jnp.where(pos == 0, 0.0, pltpu.roll(u, 1, 0))
    nxt = jnp.where(pos == seq - 1, 0.0, pltpu.roll(u, tm - 1, 0))
    w = cw_ref[...]
    y = prev * w[0:1] + u * w[1:2] + nxt * w[2:3]
    o_ref[...] = (b_gate * y).astype(o_ref.dtype)


def _conv_in(hbf, w_in, layer, conv_w):
    assert SEQ & (SEQ - 1) == 0 and DEC_SEQ & (DEC_SEQ - 1) == 0 and N_CTX % _CONV_TM == 0
    tm = _CONV_TM
    tn = 256
    per = D_MODEL // tn

    def wcol(group):
        return pl.BlockSpec((None, D_MODEL, tn), lambda i, j: (layer, 0, group * per + j))

    vm = 2 * (tm * D_MODEL * 2 + 3 * D_MODEL * tn * 2 + tm * tn * 2) + 8 * tm * tn * 4
    return pl.pallas_call(
        _conv_mm_kernel,
        grid=(N_TOK // tm, per),
        in_specs=[pl.BlockSpec((tm, D_MODEL), lambda i, j: (i, 0)), wcol(0), wcol(1), wcol(2),
                  pl.BlockSpec((CONV_W, tn), lambda i, j: (0, j))],
        out_specs=pl.BlockSpec((tm, tn), lambda i, j: (i, j)),
        out_shape=jax.ShapeDtypeStruct((N_TOK, D_MODEL), BF16),
        compiler_params=_cparams(("parallel", "parallel"), vm + (4 << 20)),
        name="conv_in",
    )(hbf, w_in, w_in, w_in, conv_w)


def _conv_mixer(hbf, w_in, w_out, layer, conv_w, x, gate):
    o = _conv_in(hbf, w_in, layer, conv_w)
    return _matmul_residual(o, w_out, layer, x, gate, _MM_TM, _MM_TN)


def _head_norm(x, w):
    return x * lax.rsqrt(jnp.mean(x * x, axis=-1, keepdims=True) + EPS) * w


def _rope(x, cos, sin_signed):
    lane = lax.broadcasted_iota(jnp.int32, x.shape, 1)
    first = (lane % ROPE_AXIS_DIM) < (ROPE_AXIS_DIM // 2)
    rot = jnp.where(first, pltpu.roll(x, HEAD_DIM - ROPE_AXIS_DIM // 2, 1), pltpu.roll(x, ROPE_AXIS_DIM // 2, 1))
    return x * cos + rot * sin_signed


def _kvprep_kernel(*refs, rope):
    if rope:
        k_ref, v_ref, w_ref, cos_ref, sin_ref, kb_ref, vb_ref = refs
    else:
        k_ref, v_ref, w_ref, kn_ref, vf_ref, kb_ref, vb_ref = refs
        vf_ref[...] = v_ref[...]
    vb_ref[...] = v_ref[...].astype(BF16)
    for h in range(N_KV_HEADS):
        cols = slice(h * HEAD_DIM, (h + 1) * HEAD_DIM)
        kn = _head_norm(k_ref[:, cols], w_ref[...])
        if rope:
            kn = _rope(kn, cos_ref[...], sin_ref[...])
        else:
            kn_ref[:, cols] = kn
        kb_ref[:, cols] = kn.astype(BF16)


def _kv_prep(qkv, k_norm, rows0, nrows, rope_tabs):
    tm = 512
    kw = N_KV_HEADS * HEAD_DIM
    kcol = (N_HEADS * HEAD_DIM) // kw
    rb0 = rows0 // tm
    in_specs = [pl.BlockSpec((tm, kw), lambda i: (rb0 + i, kcol)),
                pl.BlockSpec((tm, kw), lambda i: (rb0 + i, kcol + 1)),
                pl.BlockSpec((1, HEAD_DIM), lambda i: (0, 0))]
    args = [qkv, qkv, k_norm.reshape(1, HEAD_DIM)]
    o_spec = pl.BlockSpec((tm, kw), lambda i: (i, 0))
    out_specs = [o_spec, o_spec]
    out_shape = [jax.ShapeDtypeStruct((nrows, kw), BF16)] * 2
    if rope_tabs is not None:
        per = DEC_SEQ // tm
        in_specs += [pl.BlockSpec((tm, HEAD_DIM), lambda i: (i % per, 0))] * 2
        args += list(rope_tabs)
    else:
        out_specs = [o_spec, o_spec] + out_specs
        out_shape = [jax.ShapeDtypeStruct((nrows, kw), F32)] * 2 + out_shape
    return pl.pallas_call(
        functools.partial(_kvprep_kernel, rope=rope_tabs is not None),
        grid=(nrows // tm,),
        in_specs=in_specs,
        out_specs=out_specs,
        out_shape=out_shape,
        compiler_params=_cparams(("parallel",), 16 * tm * kw * 4),
        name="kv_prep",
    )(*args)


def _attn_kernel(*refs, rope):
    if rope:
        q_ref, k_ref, v_ref, w_ref, cos_ref, sin_ref, o_ref = refs
    else:
        q_ref, k_ref, v_ref, w_ref, o_ref = refs
    kk = k_ref[...]
    vv = v_ref[...]
    scale = HEAD_DIM ** -0.5 * math.log2(math.e)
    for g in range(GQA_GROUP):
        cols = slice(g * HEAD_DIM, (g + 1) * HEAD_DIM)
        q = _head_norm(q_ref[:, cols], w_ref[...])
        if rope:
            q = _rope(q, cos_ref[...], sin_ref[...])
        s = _dot_nt((q * scale).astype(BF16), kk)
        p = jnp.exp2(s - jnp.max(s, axis=-1, keepdims=True))
        den = jnp.sum(p, axis=-1, keepdims=True)
        o = _dot(p.astype(BF16), vv) / den
        o_ref[:, cols] = o.astype(o_ref.dtype)


def _attention(qkv, kb, vb, q_norm, rows0, seq, nb, tk, rope_tabs):
    tq = 256
    gw = GQA_GROUP * HEAD_DIM
    nq = seq // tq
    rb0 = rows0 // tq
    in_specs = [pl.BlockSpec((tq, gw), lambda b, h, i: (rb0 + b * nq + i, h)),
                pl.BlockSpec((tk, HEAD_DIM), lambda b, h, i: (b, h)),
                pl.BlockSpec((tk, HEAD_DIM), lambda b, h, i: (b, h)),
                pl.BlockSpec((1, HEAD_DIM), lambda b, h, i: (0, 0))]
    args = [qkv, kb, vb, q_norm.reshape(1, HEAD_DIM)]
    if rope_tabs is not None:
        in_specs += [pl.BlockSpec((tq, HEAD_DIM), lambda b, h, i: (i, 0))] * 2
        args += list(rope_tabs)
    vm = 2 * (tq * gw * 4 + 2 * tk * HEAD_DIM * 2 + tq * gw * 2) + 4 * tq * tk * 4 + (4 << 20)
    return pl.pallas_call(
        functools.partial(_attn_kernel, rope=rope_tabs is not None),
        grid=(nb, N_KV_HEADS, nq),
        in_specs=in_specs,
        out_specs=pl.BlockSpec((tq, gw), lambda b, h, i: (b * nq + i, h)),
        out_shape=jax.ShapeDtypeStruct((nb * seq, D_MODEL), BF16),
        compiler_params=_cparams(("parallel", "parallel", "parallel"), vm),
        name="attention",
    )(*args)


def _rope_tables():
    n_rows = DEC_SEQ // GRID_W
    row = jnp.repeat(jnp.arange(n_rows), GRID_W).astype(F32)
    colp = jnp.tile(jnp.arange(GRID_W), n_rows).astype(F32)
    inv = ROPE_THETA ** (-jnp.arange(0, ROPE_AXIS_DIM, 2, dtype=F32) / ROPE_AXIS_DIM)
    ang_r = row[:, None] * inv
    ang_c = colp[:, None] * inv
    ang = jnp.concatenate([ang_r, ang_r, ang_c, ang_c], axis=-1)
    quarter = ROPE_AXIS_DIM // 2
    sign = jnp.where((jnp.arange(HEAD_DIM) % ROPE_AXIS_DIM) < quarter, -1.0, 1.0).astype(F32)
    return jnp.cos(ang), jnp.sin(ang) * sign


def _attn_mixer(hbf, w_qkv, w_out, layer, q_norm, k_norm, cache_k, cache_v, x, gate):
    qkv = _matmul(hbf, w_qkv, layer, _MM_TM, _MM_TN)
    kw = N_KV_HEADS * HEAD_DIM
    tabs = _rope_tables()
    kn_ctx, v_ctx, kb_ctx, vb_ctx = _kv_prep(qkv, k_norm, 0, N_CTX, None)
    kb_lat, vb_lat = _kv_prep(qkv, k_norm, N_CTX, N_LAT, tabs)
    o_ctx = _attention(qkv, kb_ctx, vb_ctx, q_norm, 0, SEQ, BATCH, SEQ, None)
    tk = PAST_LEN + DEC_SEQ
    keys = jnp.concatenate([cache_k.reshape(DEC_BATCH, PAST_LEN, kw).astype(BF16),
                            kb_lat.reshape(DEC_BATCH, DEC_SEQ, kw)], axis=1).reshape(DEC_BATCH * tk, kw)
    vals = jnp.concatenate([cache_v.reshape(DEC_BATCH, PAST_LEN, kw).astype(BF16),
                            vb_lat.reshape(DEC_BATCH, DEC_SEQ, kw)], axis=1).reshape(DEC_BATCH * tk, kw)
    o_lat = _attention(qkv, keys, vals, q_norm, N_CTX, DEC_SEQ, DEC_BATCH, tk, tabs)
    o = jnp.concatenate([o_ctx, o_lat], axis=0)
    new_k = kn_ctx.reshape(BATCH, SEQ, N_KV_HEADS, HEAD_DIM)
    new_v = v_ctx.reshape(BATCH, SEQ, N_KV_HEADS, HEAD_DIM)
    return _matmul_residual(o, w_out, layer, x, gate, _MM_TM, _MM_TN), new_k, new_v


def kernel(x_prompt, x_sample, state_hgrn, cache_k, cache_v, c, c_ctx, ada_w, ada_b, norm_w, final_norm_w,
           hgrn_w_in, hgrn_lb_logits, hgrn_norm_w, hgrn_w_out, conv_w_in, conv_w, conv_w_out,
           attn_w_qkv, attn_q_norm, attn_k_norm, attn_w_out, ffn_w_gate_up, ffn_w_down,
           moe_w_router, moe_w_gate_up, moe_w_down):
    x = jnp.concatenate([x_prompt.reshape(N_CTX, D_MODEL), x_sample.reshape(N_LAT, D_MODEL)], axis=0)
    conds = jnp.zeros((COND_PAD, D_MODEL), F32).at[0].set(c_ctx).at[1:N_COND].set(c)
    mod = _modulation(conds, ada_w, ada_b)
    mod = mod.reshape(DEPTH, COND_PAD, 6, 1, D_MODEL).transpose(0, 2, 1, 3, 4)

    probs = jax.nn.softmax(hgrn_lb_logits.astype(F32), axis=0)
    csum = jnp.cumsum(probs, axis=0)
    lower_bounds = csum - csum[:1]

    ffn_gu, ffn_down = ffn_w_gate_up.astype(BF16), ffn_w_down.astype(BF16)
    moe_gu, moe_down = moe_w_gate_up.astype(BF16), moe_w_down.astype(BF16)
    hgrn_w_in, hgrn_w_out = hgrn_w_in.astype(BF16), hgrn_w_out.astype(BF16)
    conv_w_in, conv_w_out = conv_w_in.astype(BF16), conv_w_out.astype(BF16)
    attn_w_qkv, attn_w_out = attn_w_qkv.astype(BF16), attn_w_out.astype(BF16)

    new_states, new_k, new_v = [], None, None
    for layer in range(DEPTH):
        kind = layer % N_MIXERS
        j = layer // N_MIXERS
        sh1, sc1, g1, sh2, sc2, g2 = (mod[layer, t] for t in range(6))
        hbf = _norm_mod(x, norm_w[layer, 0], sc1, sh1)
        if kind == 0:
            x, st = _hgrn_mixer(hbf, hgrn_w_in, hgrn_w_out, j, lower_bounds[j], hgrn_norm_w[j],
                                state_hgrn[:, j], x, g1)
            new_states.append(st)
        elif kind == 1:
            x = _conv_mixer(hbf, conv_w_in, conv_w_out, j, conv_w[j], x, g1)
        else:
            x, new_k, new_v = _attn_mixer(hbf, attn_w_qkv, attn_w_out, j, attn_q_norm[j], attn_k_norm[j],
                                          cache_k[:, j], cache_v[:, j], x, g1)
        f = layer // 2
        if layer % 2 == 0:
            hbf = _norm_mod(x, norm_w[layer, 1], sc2, sh2)
            x = _dense_ffn(hbf, ffn_gu, ffn_down, f, x, g2)
        else:
            x = _moe_ffn(x, norm_w[layer, 1], sc2, sh2, g2, moe_w_router[f], moe_gu, moe_down, f)
    y_prompt = _final_norm(x, final_norm_w, 0, N_CTX).reshape(BATCH, SEQ, D_MODEL)
    y_sample = _final_norm(x, final_norm_w, N_CTX, N_LAT).reshape(DEC_BATCH, DEC_SEQ, D_MODEL)
    new_state_hgrn = jnp.stack(new_states, axis=1)
    return (y_prompt, y_sample, new_state_hgrn, new_k[:, None], new_v[:, None])
```

```python
import functools
import math

import numpy as np
import jax
import jax.numpy as jnp
from jax import lax
from jax.experimental import pallas as pl
from jax.experimental.pallas import tpu as pltpu

F32 = jnp.float32
BF16 = jnp.bfloat16

D_MODEL = 2048
BATCH = 32
SEQ = 256
DEPTH = 4
DEC_BATCH = 8
DEC_SEQ = 2048
PAST_LEN = 512
GRID_W = 64
N_MIXERS = 3
EPS = 1e-6
HG_HEADS = 16
HG_DK = 128
HG_DV = 128
HG_FDIM = HG_HEADS * HG_DK
CONV_W = 3
HEAD_DIM = 128
N_HEADS = 16
N_KV_HEADS = 4
GQA_GROUP = 4
ROPE_THETA = 10000.0
ROPE_AXIS_DIM = HEAD_DIM // 2
D_FF = 5632
N_EXPERTS = 8
TOP_K = 2

N_CTX = BATCH * SEQ
N_LAT = DEC_BATCH * DEC_SEQ
N_TOK = N_CTX + N_LAT
N_COND = 1 + DEC_BATCH
COND_PAD = 16

V7X_LANES = 128
LANE_CHUNKS = D_MODEL // V7X_LANES
V7X_VMEM_BYTES = 64 * 1024 * 1024
VMEM_CAP = 56 * 1024 * 1024

HG_CHUNK = 64
HG_LEVELS = int(math.log2(HG_CHUNK))
HG_TOT_ROWS = 16
HG_SPLIT = 2
HG_PAIR_W = 2 * HG_DK
HG_TB = 256


def _cparams(sem, vmem_bytes):
    return pltpu.CompilerParams(dimension_semantics=sem,
                                vmem_limit_bytes=int(min(VMEM_CAP, max(vmem_bytes, 16 * 1024 * 1024))))


def _cond_of_tile(i, tm):
    r = i * tm
    return jnp.where(r < N_CTX, 0, 1 + (r - N_CTX) // DEC_SEQ)


def _silu(x):
    return x * jax.nn.sigmoid(x)


def _dot(a, b):
    return jnp.dot(a, b, preferred_element_type=F32)


def _dot_nt(a, b):
    return lax.dot_general(a, b, (((1,), (1,)), ((), ())), preferred_element_type=F32)


def _dot_tn(a, b):
    return lax.dot_general(a, b, (((0,), (0,)), ((), ())), preferred_element_type=F32)


def _mod_kernel(c_ref, w_ref, b_ref, o_ref):
    a = _silu(c_ref[...]).astype(BF16)
    o_ref[...] = _dot(a, w_ref[...].astype(BF16)) + b_ref[...]


def _modulation(conds, ada_w, ada_b):
    tn = 1024
    n_out = 6 * D_MODEL
    return pl.pallas_call(
        _mod_kernel,
        grid=(DEPTH, n_out // tn),
        in_specs=[pl.BlockSpec((COND_PAD, D_MODEL), lambda l, j: (0, 0)),
                  pl.BlockSpec((None, D_MODEL, tn), lambda l, j: (l, 0, j)),
                  pl.BlockSpec((None, 1, tn), lambda l, j: (l, 0, j))],
        out_specs=pl.BlockSpec((None, COND_PAD, tn), lambda l, j: (l, 0, j)),
        out_shape=jax.ShapeDtypeStruct((DEPTH, COND_PAD, n_out), F32),
        compiler_params=_cparams(("parallel", "parallel"), 3 * D_MODEL * tn * 4 + (4 << 20)),
        name="modulation",
    )(conds, ada_w, ada_b.reshape(DEPTH, 1, n_out))


def _norm_body(x_ref, w_ref, sc_ref, sh_ref):
    x = x_ref[...]
    y = x * lax.rsqrt(jnp.mean(x * x, axis=-1, keepdims=True) + EPS) * w_ref[...]
    return y * (1.0 + sc_ref[...]) + sh_ref[...]


def _norm_mod_kernel(x_ref, w_ref, sc_ref, sh_ref, o_ref):
    o_ref[...] = _norm_body(x_ref, w_ref, sc_ref, sh_ref).astype(o_ref.dtype)


def _final_norm_kernel(x_ref, w_ref, o_ref):
    x = x_ref[...]
    o_ref[...] = x * lax.rsqrt(jnp.mean(x * x, axis=-1, keepdims=True) + EPS) * w_ref[...]


def _norm_router_kernel(x_ref, w_ref, sc_ref, sh_ref, rh_ref, rl_ref, h_ref, r_ref):
    h = _norm_body(x_ref, w_ref, sc_ref, sh_ref)
    tm = h.shape[0]
    for cc in range(LANE_CHUNKS):
        h_ref[pl.ds(cc, tm, stride=LANE_CHUNKS), :] = h[:, cc * V7X_LANES:(cc + 1) * V7X_LANES]
    hh = h.astype(BF16)
    hl = (h - hh.astype(F32)).astype(BF16)
    logits = _dot(hh, rh_ref[...]) + (_dot(hh, rl_ref[...]) + _dot(hl, rh_ref[...]))
    lane = lax.broadcasted_iota(jnp.int32, logits.shape, 1)
    neg = jnp.float32(-jnp.inf)
    logits = jnp.where(lane < N_EXPERTS, logits, neg)
    v1 = jnp.max(logits, axis=-1, keepdims=True)
    i1 = jnp.min(jnp.where(logits == v1, lane, V7X_LANES), axis=-1, keepdims=True)
    rest = jnp.where(lane == i1, neg, logits)
    v2 = jnp.max(rest, axis=-1, keepdims=True)
    i2 = jnp.min(jnp.where(rest == v2, lane, V7X_LANES), axis=-1, keepdims=True)
    e = jnp.exp(v2 - v1)
    g1 = 1.0 / (1.0 + e)
    g2 = e * g1
    out = jnp.where(lane == 0, i1.astype(F32), 0.0)
    out = jnp.where(lane == 1, i2.astype(F32), out)
    out = jnp.where(lane == 2, g1, out)
    out = jnp.where(lane == 3, g2, out)
    r_ref[...] = out


_NORM_TM = 512


def _row_specs(tm):
    x_spec = pl.BlockSpec((tm, D_MODEL), lambda i: (i, 0))
    w_spec = pl.BlockSpec((1, D_MODEL), lambda i: (0, 0))
    c_spec = pl.BlockSpec((None, 1, D_MODEL), lambda i: (_cond_of_tile(i, tm), 0, 0))
    return x_spec, w_spec, c_spec


def _norm_mod(x, w, sc, sh, out_dtype=BF16):
    tm = _NORM_TM
    x_spec, w_spec, c_spec = _row_specs(tm)
    return pl.pallas_call(
        _norm_mod_kernel,
        grid=(N_TOK // tm,),
        in_specs=[x_spec, w_spec, c_spec, c_spec],
        out_specs=x_spec,
        out_shape=jax.ShapeDtypeStruct((N_TOK, D_MODEL), out_dtype),
        compiler_params=_cparams(("parallel",), 6 * tm * D_MODEL * 4),
        name="norm_mod",
    )(x, w.reshape(1, D_MODEL), sc, sh)


def _final_norm(x, w, rows0, nrows):
    tm = _NORM_TM
    rb0 = rows0 // tm
    return pl.pallas_call(
        _final_norm_kernel,
        grid=(nrows // tm,),
        in_specs=[pl.BlockSpec((tm, D_MODEL), lambda i: (rb0 + i, 0)),
                  pl.BlockSpec((1, D_MODEL), lambda i: (0, 0))],
        out_specs=pl.BlockSpec((tm, D_MODEL), lambda i: (i, 0)),
        out_shape=jax.ShapeDtypeStruct((nrows, D_MODEL), F32),
        compiler_params=_cparams(("parallel",), 6 * tm * D_MODEL * 4),
        name="final_norm",
    )(x, w.reshape(1, D_MODEL))


def _norm_router(x, w, sc, sh, w_router):
    tm = _NORM_TM
    x_spec, w_spec, c_spec = _row_specs(tm)
    wr = jnp.zeros((D_MODEL, V7X_LANES), F32).at[:, :N_EXPERTS].set(w_router)
    wr_hi = wr.astype(BF16)
    wr_lo = (wr - wr_hi.astype(F32)).astype(BF16)
    r_spec = pl.BlockSpec((D_MODEL, V7X_LANES), lambda i: (0, 0))
    return pl.pallas_call(
        _norm_router_kernel,
        grid=(N_TOK // tm,),
        in_specs=[x_spec, w_spec, c_spec, c_spec, r_spec, r_spec],
        out_specs=[pl.BlockSpec((tm * LANE_CHUNKS, V7X_LANES), lambda i: (i, 0)),
                   pl.BlockSpec((tm, V7X_LANES), lambda i: (i, 0))],
        out_shape=[jax.ShapeDtypeStruct((N_TOK * LANE_CHUNKS, V7X_LANES), F32),
                   jax.ShapeDtypeStruct((N_TOK, V7X_LANES), F32)],
        compiler_params=_cparams(("parallel",), 8 * tm * D_MODEL * 4),
        name="norm_router",
    )(x, w.reshape(1, D_MODEL), sc, sh, wr_hi, wr_lo)


_MM_TM = 1024
_MM_TN = 1024


def _mm_kernel(a_ref, w_ref, o_ref):
    o_ref[...] = _dot(a_ref[...], w_ref[...]).astype(o_ref.dtype)


def _mm_res_kernel(a_ref, w_ref, x_ref, g_ref, o_ref):
    o_ref[...] = x_ref[...] + g_ref[...] * _dot(a_ref[...], w_ref[...])


def _matmul(a, w_all, layer, tm, tn, out_dtype=F32):
    m, k = a.shape
    n = w_all.shape[2]
    wb = w_all.dtype.itemsize
    vm = 2 * (tm * k * 2 + k * tn * wb + tm * tn * 4) + tm * tn * 4 + k * tn * 2
    return pl.pallas_call(
        _mm_kernel,
        grid=(m // tm, n // tn),
        in_specs=[pl.BlockSpec((tm, k), lambda i, j: (i, 0)),
                  pl.BlockSpec((None, k, tn), lambda i, j: (layer, 0, j))],
        out_specs=pl.BlockSpec((tm, tn), lambda i, j: (i, j)),
        out_shape=jax.ShapeDtypeStruct((m, n), out_dtype),
        compiler_params=_cparams(("parallel", "parallel"), vm + (4 << 20)),
        name="matmul",
    )(a, w_all)


def _matmul_residual(a, w_all, layer, x, gate, tm, tn):
    m, k = a.shape
    n = w_all.shape[2]
    wb = w_all.dtype.itemsize
    vm = 2 * (tm * k * 2 + k * tn * wb + 2 * tm * tn * 4) + tm * tn * 4 + k * tn * 2
    return pl.pallas_call(
        _mm_res_kernel,
        grid=(m // tm, n // tn),
        in_specs=[pl.BlockSpec((tm, k), lambda i, j: (i, 0)),
                  pl.BlockSpec((None, k, tn), lambda i, j: (layer, 0, j)),
                  pl.BlockSpec((tm, tn), lambda i, j: (i, j)),
                  pl.BlockSpec((None, 1, tn), lambda i, j: (_cond_of_tile(i, tm), 0, j))],
        out_specs=pl.BlockSpec((tm, tn), lambda i, j: (i, j)),
        out_shape=jax.ShapeDtypeStruct((m, n), F32),
        compiler_params=_cparams(("parallel", "parallel"), vm + (4 << 20)),
        name="matmul_residual",
    )(a, w_all, x, gate)


_FFN_TF = 512


def _ffn_partial(x_ref, wa_ref, wg_ref, wd_ref):
    x = x_ref[...]
    a = _dot(x, wa_ref[...])
    g = _dot(x, wg_ref[...])
    hid = (_silu(a) * g).astype(BF16)
    return _dot(hid, wd_ref[...])


def _ffn_moe_kernel(te_ref, nu_ref, rt_ref, h_hbm, wa_ref, wg_ref, wd_ref, o_ref, stage_a, stage_b, xb, sem):
    i = pl.program_id(0)
    f = pl.program_id(1)
    nt = pl.num_programs(0)
    nf = pl.num_programs(1)
    tm = o_ref.shape[0]
    lc = LANE_CHUNKS

    def issue(tile, dst_buf, dst_sem, r):
        tok = rt_ref[tile * tm + r]
        src = h_hbm.at[pl.ds(pl.multiple_of(tok * lc, lc), lc), :]
        pltpu.make_async_copy(src, dst_buf.at[pl.ds(r * lc, lc), :], sem.at[dst_sem]).start()

    def wait(buf, on_sem):
        pltpu.make_async_copy(h_hbm.at[pl.ds(0, tm * lc), :], buf, sem.at[on_sem]).wait()

    @pl.when((i == 0) & (f == 0))
    def _():
        def body(r, carry):
            issue(0, stage_a, 0, r)
            return carry
        lax.fori_loop(0, tm, body, 0, unroll=_DMA_ISSUE_UNROLL)

    nxt = jnp.minimum(i + 1, nt - 1)
    for parity, (cur, other) in enumerate(((stage_a, stage_b), (stage_b, stage_a))):
        @pl.when((f == 0) & (i % 2 == parity))
        def _(parity=parity, cur=cur, other=other):
            wait(cur, parity)
            for cc in range(lc):
                for r in range(cc * tm // lc, (cc + 1) * tm // lc):
                    issue(nxt, other, 1 - parity, r)
                xb[:, cc * V7X_LANES:(cc + 1) * V7X_LANES] = cur[pl.ds(cc, tm, stride=lc), :].astype(BF16)
            o_ref[...] = jnp.zeros(o_ref.shape, o_ref.dtype)

    @pl.when(i < nu_ref[0])
    def _():
        o_ref[...] += _ffn_partial(xb, wa_ref, wg_ref, wd_ref)

    for parity, other in enumerate((stage_b, stage_a)):
        @pl.when((i == nt - 1) & (f == nf - 1) & (i % 2 == parity))
        def _(parity=parity, other=other):
            wait(other, 1 - parity)


def _moe_ffn_call(h, row_tok, w_gu, w_down, layer, tile_expert, n_used):
    tm = _MOE_TM
    tf = _FFN_TF
    nf = D_FF // tf
    vm = (2 * tm * D_MODEL * 4 + tm * D_MODEL * 2 + 2 * (3 * D_MODEL * tf * 2 + tm * D_MODEL * 4)
          + tm * D_MODEL * 4 + 3 * tm * tf * 4)
    return pl.pallas_call(
        _ffn_moe_kernel,
        grid_spec=pltpu.PrefetchScalarGridSpec(
            num_scalar_prefetch=3,
            grid=(_MOE_ROWS // tm, nf),
            in_specs=[pl.BlockSpec(memory_space=pl.ANY),
                      pl.BlockSpec((None, None, D_MODEL, tf), lambda i, f, te, nu, rt: (layer, te[i], 0, f)),
                      pl.BlockSpec((None, None, D_MODEL, tf), lambda i, f, te, nu, rt: (layer, te[i], 0, f + nf)),
                      pl.BlockSpec((None, None, tf, D_MODEL), lambda i, f, te, nu, rt: (layer, te[i], f, 0))],
            out_specs=pl.BlockSpec((tm, D_MODEL), lambda i, f, te, nu, rt: (i, 0)),
            scratch_shapes=[pltpu.VMEM((tm * LANE_CHUNKS, V7X_LANES), F32),
                            pltpu.VMEM((tm * LANE_CHUNKS, V7X_LANES), F32),
                            pltpu.VMEM((tm, D_MODEL), BF16),
                            pltpu.SemaphoreType.DMA((2,))]),
        out_shape=jax.ShapeDtypeStruct((_MOE_ROWS, D_MODEL), F32),
        compiler_params=_cparams(("arbitrary", "arbitrary"), vm + (4 << 20)),
        name="moe_swiglu",
    )(tile_expert, n_used, row_tok, h, w_gu, w_gu, w_down)


def _ffn_res_kernel(x_ref, wa_ref, wg_ref, wd_ref, r_ref, g_ref, o_ref):
    @pl.when(pl.program_id(1) == 0)
    def _():
        o_ref[...] = r_ref[...]

    o_ref[...] += g_ref[...] * _ffn_partial(x_ref, wa_ref, wg_ref, wd_ref)


def _dense_ffn(hb, w_gu, w_down, layer, x, gate):
    tm = 512
    tf = _FFN_TF
    nf = D_FF // tf
    vm = 2 * (tm * D_MODEL * 2 + 3 * D_MODEL * tf * 2 + 2 * tm * D_MODEL * 4) + tm * D_MODEL * 4 + 3 * tm * tf * 4
    return pl.pallas_call(
        _ffn_res_kernel,
        grid=(N_TOK // tm, nf),
        in_specs=[pl.BlockSpec((tm, D_MODEL), lambda i, f: (i, 0)),
                  pl.BlockSpec((None, D_MODEL, tf), lambda i, f: (layer, 0, f)),
                  pl.BlockSpec((None, D_MODEL, tf), lambda i, f: (layer, 0, f + nf)),
                  pl.BlockSpec((None, tf, D_MODEL), lambda i, f: (layer, f, 0)),
                  pl.BlockSpec((tm, D_MODEL), lambda i, f: (i, 0)),
                  pl.BlockSpec((None, 1, D_MODEL), lambda i, f: (_cond_of_tile(i, tm), 0, 0))],
        out_specs=pl.BlockSpec((tm, D_MODEL), lambda i, f: (i, 0)),
        out_shape=jax.ShapeDtypeStruct((N_TOK, D_MODEL), F32),
        compiler_params=_cparams(("parallel", "arbitrary"), vm + (4 << 20)),
        name="swiglu",
    )(hb, w_gu, w_gu, w_down, x, gate)


_MOE_TM = 512
_MOE_ROWS = TOP_K * N_TOK + N_EXPERTS * _MOE_TM
_GATHER_TM = 256
_DMA_ISSUE_UNROLL = 8


def _combine_kernel(d0_ref, d1_ref, y_hbm, x_ref, gt_ref, w_ref, o_ref, buf, sem):
    i = pl.program_id(0)
    tm = o_ref.shape[0]

    def issue(tile, slot):
        def body(r, carry):
            t = tile * tm + r
            pltpu.make_async_copy(y_hbm.at[pl.ds(d0_ref[t], 1), :], buf.at[slot, 0, pl.ds(r, 1), :],
                                  sem.at[slot]).start()
            pltpu.make_async_copy(y_hbm.at[pl.ds(d1_ref[t], 1), :], buf.at[slot, 1, pl.ds(r, 1), :],
                                  sem.at[slot]).start()
            return carry
        lax.fori_loop(0, tm, body, 0, unroll=_DMA_ISSUE_UNROLL)

    @pl.when(i == 0)
    def _():
        issue(0, 0)

    @pl.when(i + 1 < pl.num_programs(0))
    def _():
        issue(i + 1, (i + 1) % 2)

    slot = i % 2
    pltpu.make_async_copy(y_hbm.at[pl.ds(0, tm), :], buf.at[slot, 0], sem.at[slot]).wait()
    pltpu.make_async_copy(y_hbm.at[pl.ds(0, tm), :], buf.at[slot, 1], sem.at[slot]).wait()
    w = w_ref[...]
    y = w[:, 2:3] * buf[slot, 0] + w[:, 3:4] * buf[slot, 1]
    o_ref[...] = x_ref[...] + gt_ref[...] * y


def _combine_rows(y, d0, d1, x, gate, route):
    tm = _GATHER_TM
    return pl.pallas_call(
        _combine_kernel,
        grid_spec=pltpu.PrefetchScalarGridSpec(
            num_scalar_prefetch=2,
            grid=(N_TOK // tm,),
            in_specs=[pl.BlockSpec(memory_space=pl.ANY),
                      pl.BlockSpec((tm, D_MODEL), lambda i, a, b: (i, 0)),
                      pl.BlockSpec((None, 1, D_MODEL), lambda i, a, b: (_cond_of_tile(i, tm), 0, 0)),
                      pl.BlockSpec((tm, V7X_LANES), lambda i, a, b: (i, 0))],
            out_specs=pl.BlockSpec((tm, D_MODEL), lambda i, a, b: (i, 0)),
            scratch_shapes=[pltpu.VMEM((2, 2, tm, D_MODEL), F32), pltpu.SemaphoreType.DMA((2,))]),
        out_shape=jax.ShapeDtypeStruct((N_TOK, D_MODEL), F32),
        compiler_params=_cparams(("arbitrary",), 10 * tm * D_MODEL * 4),
        name="moe_combine",
    )(d0, d1, y, x, gate, route)


def _moe_ffn(x, norm_w, sc, sh, gate, w_router, w_gu, w_down, layer):
    h, route = _norm_router(x, norm_w, sc, sh, w_router)
    top_i = route[:, :TOP_K].astype(jnp.int32)
    member = jnp.sum(top_i[:, :, None] == jnp.arange(N_EXPERTS, dtype=jnp.int32), axis=1, dtype=jnp.int32)
    before = jnp.cumsum(member, axis=0) - member
    counts = before[-1] + member[-1]
    padded = ((counts + _MOE_TM - 1) // _MOE_TM) * _MOE_TM
    ends = jnp.cumsum(padded)
    starts = ends - padded
    dest = jnp.take(starts, top_i) + jnp.take_along_axis(before, top_i, axis=1)
    tok = jnp.broadcast_to(jnp.arange(N_TOK, dtype=jnp.int32)[:, None], (N_TOK, TOP_K))
    row_tok = jnp.zeros((_MOE_ROWS,), jnp.int32).at[dest.reshape(-1)].set(tok.reshape(-1))
    n_tiles = _MOE_ROWS // _MOE_TM
    tile_start = jnp.arange(n_tiles, dtype=jnp.int32) * _MOE_TM
    tile_expert = jnp.minimum(jnp.sum(tile_start[:, None] >= ends[None, :], axis=1), N_EXPERTS - 1).astype(jnp.int32)
    n_used = (ends[-1:] // _MOE_TM).astype(jnp.int32)
    ys = _moe_ffn_call(h, row_tok, w_gu, w_down, layer, tile_expert, n_used)
    return _combine_rows(ys, dest[:, 0], dest[:, 1], x, gate, route)


@functools.lru_cache(maxsize=None)
def _hgrn_tables():
    c = HG_CHUNK
    n_rows = c * (HG_LEVELS + 1) + HG_TOT_ROWS
    prefix = np.zeros((2, n_rows, c), np.float32)
    role = np.zeros((2, HG_LEVELS, c, 1), np.float32)
    pair = np.zeros((2, HG_LEVELS, c, c), np.float32)
    for d in range(2):
        for i in range(c):
            if d == 0:
                prefix[d, i, :i + 1] = 1.0
            else:
                prefix[d, i, i:] = 1.0
        for l in range(HG_LEVELS):
            m = 2 ** l
            for i in range(c):
                blk = i // (2 * m)
                upper = (i // m) % 2 == 1
                mid = blk * 2 * m + m
                row = c * (l + 1) + i
                if d == 0:
                    is_q = upper
                    lo, hi = (mid, i + 1) if upper else (i + 1, mid)
                else:
                    is_q = not upper
                    lo, hi = (mid, i) if upper else (i, mid)
                prefix[d, row, lo:hi] = 1.0
                role[d, l, i, 0] = 1.0 if is_q else 0.0
            for i in range(c):
                for j in range(c):
                    same = i // (2 * m) == j // (2 * m)
                    if same and role[d, l, i, 0] == 1.0 and role[d, l, j, 0] == 0.0:
                        pair[d, l, i, j] = 1.0
        prefix[d, c * (HG_LEVELS + 1):, :] = 1.0
    prefix = np.concatenate([prefix] * HG_SPLIT, axis=2)
    pair = np.concatenate([pair, pair], axis=3)
    return prefix, pair


def _pair_blockdiag(x2):
    zero = jnp.zeros((x2.shape[0], HG_DK), x2.dtype)
    return jnp.concatenate([jnp.concatenate([x2[:, :HG_DK], zero], axis=1),
                            jnp.concatenate([zero, x2[:, HG_DK:]], axis=1)], axis=0)


def _role_select(qh, k, level, backward):
    m = 2 ** level
    if m % 8 == 0:
        parts = []
        for r in range(0, HG_CHUNK, m):
            upper = (r // m) % 2 == 1
            parts.append((qh if upper != backward else k)[r:r + m])
        return jnp.concatenate(parts, axis=0)
    row = lax.broadcasted_iota(jnp.int32, qh.shape, 0)
    upper = (row // m) % 2 == 1
    return jnp.where(upper != backward, qh, k)


def _hgrn_dir_kernel(*refs, backward, hb, has_init, write_state):
    it = iter(refs)
    q_ref, f_ref, v_ref, la_ref, l1_ref, pm_ref, pair_ref = (next(it) for _ in range(7))
    s0_ref = next(it) if has_init else None
    if backward:
        of_ref, gt_ref, nw_ref = next(it), next(it), next(it)
    o_ref = next(it)
    st_ref = next(it) if write_state else None
    st_scr = next(it)

    c = HG_CHUNK
    n_chunk = HG_TB // c
    n_pair = hb // 2
    t = pl.program_id(2)

    @pl.when(t == 0)
    def _():
        for p in range(n_pair):
            if has_init:
                st_scr[p] = jnp.concatenate([s0_ref[2 * p].T, s0_ref[2 * p + 1].T], axis=1)
            else:
                st_scr[p] = jnp.zeros((HG_DV, HG_PAIR_W), F32)

    def pair_unit(p, ci):
        r0 = ci * c
        cols = slice(p * HG_PAIR_W, (p + 1) * HG_PAIR_W)
        qh = _silu(q_ref[pl.ds(r0, c), cols])
        z = f_ref[pl.ds(r0, c), cols]
        v = v_ref[pl.ds(r0, c), cols]
        la = la_ref[:, cols]
        b = l1_ref[:, cols] + (jnp.minimum(z, 0.0) - jnp.log(1.0 + jnp.exp(-jnp.abs(z))))
        g = jnp.maximum(la, b) + jnp.log(1.0 + jnp.exp(-jnp.abs(la - b)))
        g = g * math.log2(math.e)
        k = 1.0 - jnp.exp2(g)
        g1 = g.astype(BF16)
        g2 = (g - g1.astype(F32)).astype(BF16)
        sums = _dot(pm_ref[...], jnp.concatenate([g1, g2], axis=0))
        cum = sums[0:c]
        tot = sums[c * (HG_LEVELS + 1):c * (HG_LEVELS + 1) + 1]
        scores = jnp.zeros((c, HG_PAIR_W // 2), F32)
        for l in range(HG_LEVELS):
            e = jnp.exp2(sums[c * (l + 1):c * (l + 2)])
            xk = (_role_select(qh, k, l, backward) * e).astype(BF16)
            scores = scores + pair_ref[l] * _dot_nt(xk, _pair_blockdiag(xk))
        vb = v.astype(BF16)
        qk = qh * k
        lane = lax.broadcasted_iota(jnp.int32, qk.shape, 1)
        self_score = jnp.where(lane < HG_DK, jnp.sum(qk[:, :HG_DK], axis=-1, keepdims=True),
                               jnp.sum(qk[:, HG_DK:], axis=-1, keepdims=True))
        o = _dot(scores.astype(BF16), _pair_blockdiag(vb)) + self_score * v
        st = st_scr[p]
        o = o + _dot_nt((qh * jnp.exp2(cum)).astype(BF16), _pair_blockdiag(st.astype(BF16)))
        kt = (k * jnp.exp2(tot - cum)).astype(BF16)
        v_rows = jnp.concatenate([vb[:, :HG_DV], vb[:, HG_DV:]], axis=0)
        st_scr[p] = st * jnp.exp2(tot) + _dot_tn(v_rows, _pair_blockdiag(kt))
        if backward:
            o = o + of_ref[pl.ds(r0, c), cols]
            gate = _silu(gt_ref[pl.ds(r0, c), cols])
            for hh in range(2):
                hc = slice(hh * HG_DV, (hh + 1) * HG_DV)
                oh = o[:, hc]
                y = oh * lax.rsqrt(jnp.mean(oh * oh, axis=-1, keepdims=True) + EPS) * nw_ref[...]
                o_ref[pl.ds(r0, c), pl.ds(p * HG_PAIR_W + hh * HG_DV, HG_DV)] = (y * gate[:, hc]).astype(o_ref.dtype)
        else:
            o_ref[pl.ds(r0, c), cols] = o

    for i in range(n_chunk):
        ci = n_chunk - 1 - i if backward else i
        for p in range(n_pair):
            pair_unit(p, ci)

    if write_state:
        @pl.when(t == pl.num_programs(2) - 1)
        def _():
            for p in range(n_pair):
                st = st_scr[p]
                st_ref[2 * p] = st[:, :HG_DK].T
                st_ref[2 * p + 1] = st[:, HG_DK:].T


def _hgrn_scan(p, log_lb, log1m_lb, norm_w, rows0, seq, nb, hb, s0=None, write_state=False):
    w = hb * HG_DK
    ngrp = HG_HEADS // hb
    per = D_MODEL // w
    tb = HG_TB
    n_t = seq // tb
    rb0 = rows0 // tb
    prefix, pair = _hgrn_tables()
    outs = []
    o_fwd = None
    for d in range(2):
        def tblk(t, d=d):
            return n_t - 1 - t if d == 1 else t

        def col(group, d=d, tblk=tblk):
            return pl.BlockSpec((tb, w), lambda b, g, t: (rb0 + b * n_t + tblk(t), group * per + g))

        def row(d=d, tblk=tblk):
            return pl.BlockSpec((tb, w), lambda b, g, t: (b * n_t + tblk(t), g))

        in_specs = [col(0), col(1 + d), col(3),
                    pl.BlockSpec((None, 1, w), lambda b, g, t, d=d: (d, 0, g)),
                    pl.BlockSpec((None, 1, w), lambda b, g, t, d=d: (d, 0, g)),
                    pl.BlockSpec(prefix.shape[1:], lambda b, g, t: (0, 0)),
                    pl.BlockSpec(pair.shape[1:], lambda b, g, t: (0, 0, 0))]
        args = [p, p, p, log_lb, log1m_lb, jnp.asarray(prefix[d], BF16), jnp.asarray(pair[d])]
        if s0 is not None:
            in_specs.append(pl.BlockSpec((None, None, hb, HG_DK, HG_DV), lambda b, g, t, d=d: (b, d, g, 0, 0)))
            args.append(s0)
        if d == 1:
            in_specs += [row(), col(4), pl.BlockSpec((1, HG_DV), lambda b, g, t: (0, 0))]
            args += [o_fwd, p, norm_w.reshape(1, HG_DV)]
        out_specs = [row()]
        out_shape = [jax.ShapeDtypeStruct((nb * seq, D_MODEL), BF16 if d == 1 else F32)]
        if write_state:
            out_specs.append(pl.BlockSpec((None, hb, HG_DK, HG_DV), lambda b, g, t: (b, g, 0, 0)))
            out_shape.append(jax.ShapeDtypeStruct((nb, HG_HEADS, HG_DK, HG_DV), F32))
        vm = 2 * 7 * tb * w * 4 + 6 * hb * HG_DK * HG_DV * 4 + (16 << 20)
        res = pl.pallas_call(
            functools.partial(_hgrn_dir_kernel, backward=d == 1, hb=hb, has_init=s0 is not None,
                              write_state=write_state),
            grid=(nb, ngrp, n_t),
            in_specs=in_specs,
            out_specs=out_specs,
            out_shape=out_shape,
            scratch_shapes=[pltpu.VMEM((hb // 2, HG_DV, HG_PAIR_W), F32)],
            compiler_params=_cparams(("parallel", "parallel", "arbitrary"), vm),
            name="hgrn_bwd" if d == 1 else "hgrn_fwd",
        )(*args)
        if d == 0:
            o_fwd = res[0]
        outs.append(res)
    o = outs[1][0]
    if write_state:
        return o, jnp.stack([outs[0][1], outs[1][1]], axis=1)
    return (o,)


def _hgrn_mixer(hbf, w_in, w_out, layer, lb, norm_w, s0_lat, x, gate):
    p = _matmul(hbf, w_in, layer, _MM_TM, _MM_TN)
    log_lb = jnp.log(lb).reshape(2, 1, HG_FDIM)
    log1m_lb = jnp.log1p(-lb).reshape(2, 1, HG_FDIM)
    o_ctx, st = _hgrn_scan(p, log_lb, log1m_lb, norm_w, 0, SEQ, BATCH, 16, write_state=True)
    (o_lat,) = _hgrn_scan(p, log_lb, log1m_lb, norm_w, N_CTX, DEC_SEQ, DEC_BATCH, 16, s0=s0_lat)
    o = jnp.concatenate([o_ctx, o_lat], axis=0)
    return _matmul_residual(o, w_out, layer, x, gate, _MM_TM, _MM_TN), st


_CONV_TM = DEC_SEQ


def _conv_mm_kernel(a_ref, wb_ref, wc_ref, wx_ref, cw_ref, o_ref):
    a = a_ref[...]
    b_gate = _dot(a, wb_ref[...])
    u = _dot(a, wc_ref[...]) * _dot(a, wx_ref[...])
    tm = u.shape[0]
    seq = jnp.where(pl.program_id(0) * tm < N_CTX, SEQ, DEC_SEQ)
    pos = lax.broadcasted_iota(jnp.int32, u.shape, 0) & (seq - 1)
    prev = jnp.where(pos == 0, 0.0, pltpu.roll(u, 1, 0))
    nxt = jnp.where(pos == seq - 1, 0.0, pltpu.roll(u, tm - 1, 0))
    w = cw_ref[...]
    y = prev * w[0:1] + u * w[1:2] + nxt * w[2:3]
    o_ref[...] = (b_gate * y).astype(o_ref.dtype)


def _conv_in(hbf, w_in, layer, conv_w):
    assert SEQ & (SEQ - 1) == 0 and DEC_SEQ & (DEC_SEQ - 1) == 0 and N_CTX % _CONV_TM == 0
    tm = _CONV_TM
    tn = 256
    per = D_MODEL // tn

    def wcol(group):
        return pl.BlockSpec((None, D_MODEL, tn), lambda i, j: (layer, 0, group * per + j))

    vm = 2 * (tm * D_MODEL * 2 + 3 * D_MODEL * tn * 2 + tm * tn * 2) + 8 * tm * tn * 4
    return pl.pallas_call(
        _conv_mm_kernel,
        grid=(N_TOK // tm, per),
        in_specs=[pl.BlockSpec((tm, D_MODEL), lambda i, j: (i, 0)), wcol(0), wcol(1), wcol(2),
                  pl.BlockSpec((CONV_W, tn), lambda i, j: (0, j))],
        out_specs=pl.BlockSpec((tm, tn), lambda i, j: (i, j)),
        out_shape=jax.ShapeDtypeStruct((N_TOK, D_MODEL), BF16),
        compiler_params=_cparams(("parallel", "parallel"), vm + (4 << 20)),
        name="conv_in",
    )(hbf, w_in, w_in, w_in, conv_w)


def _conv_mixer(hbf, w_in, w_out, layer, conv_w, x, gate):
    o = _conv_in(hbf, w_in, layer, conv_w)
    return _matmul_residual(o, w_out, layer, x, gate, _MM_TM, _MM_TN)


def _head_norm(x, w):
    return x * lax.rsqrt(jnp.mean(x * x, axis=-1, keepdims=True) + EPS) * w


def _rope(x, cos, sin_signed):
    lane = lax.broadcasted_iota(jnp.int32, x.shape, 1)
    first = (lane % ROPE_AXIS_DIM) < (ROPE_AXIS_DIM // 2)
    rot = jnp.where(first, pltpu.roll(x, HEAD_DIM - ROPE_AXIS_DIM // 2, 1), pltpu.roll(x, ROPE_AXIS_DIM // 2, 1))
    return x * cos + rot * sin_signed


def _kvprep_kernel(*refs, rope):
    if rope:
        k_ref, v_ref, w_ref, cos_ref, sin_ref, kb_ref, vb_ref = refs
    else:
        k_ref, v_ref, w_ref, kn_ref, vf_ref, kb_ref, vb_ref = refs
        vf_ref[...] = v_ref[...]
    vb_ref[...] = v_ref[...].astype(BF16)
    for h in range(N_KV_HEADS):
        cols = slice(h * HEAD_DIM, (h + 1) * HEAD_DIM)
        kn = _head_norm(k_ref[:, cols], w_ref[...])
        if rope:
            kn = _rope(kn, cos_ref[...], sin_ref[...])
        else:
            kn_ref[:, cols] = kn
        kb_ref[:, cols] = kn.astype(BF16)


def _kv_prep(qkv, k_norm, rows0, nrows, rope_tabs):
    tm = 512
    kw = N_KV_HEADS * HEAD_DIM
    kcol = (N_HEADS * HEAD_DIM) // kw
    rb0 = rows0 // tm
    in_specs = [pl.BlockSpec((tm, kw), lambda i: (rb0 + i, kcol)),
                pl.BlockSpec((tm, kw), lambda i: (rb0 + i, kcol + 1)),
                pl.BlockSpec((1, HEAD_DIM), lambda i: (0, 0))]
    args = [qkv, qkv, k_norm.reshape(1, HEAD_DIM)]
    o_spec = pl.BlockSpec((tm, kw), lambda i: (i, 0))
    out_specs = [o_spec, o_spec]
    out_shape = [jax.ShapeDtypeStruct((nrows, kw), BF16)] * 2
    if rope_tabs is not None:
        per = DEC_SEQ // tm
        in_specs += [pl.BlockSpec((tm, HEAD_DIM), lambda i: (i % per, 0))] * 2
        args += list(rope_tabs)
    else:
        out_specs = [o_spec, o_spec] + out_specs
        out_shape = [jax.ShapeDtypeStruct((nrows, kw), F32)] * 2 + out_shape
    return pl.pallas_call(
        functools.partial(_kvprep_kernel, rope=rope_tabs is not None),
        grid=(nrows // tm,),
        in_specs=in_specs,
        out_specs=out_specs,
        out_shape=out_shape,
        compiler_params=_cparams(("parallel",), 16 * tm * kw * 4),
        name="kv_prep",
    )(*args)


def _attn_kernel(*refs, rope):
    if rope:
        q_ref, k_ref, v_ref, w_ref, cos_ref, sin_ref, o_ref = refs
    else:
        q_ref, k_ref, v_ref, w_ref, o_ref = refs
    kk = k_ref[...]
    vv = v_ref[...]
    scale = HEAD_DIM ** -0.5 * math.log2(math.e)
    for g in range(GQA_GROUP):
        cols = slice(g * HEAD_DIM, (g + 1) * HEAD_DIM)
        q = _head_norm(q_ref[:, cols], w_ref[...])
        if rope:
            q = _rope(q, cos_ref[...], sin_ref[...])
        s = _dot_nt((q * scale).astype(BF16), kk)
        p = jnp.exp2(s - jnp.max(s, axis=-1, keepdims=True))
        den = jnp.sum(p, axis=-1, keepdims=True)
        o = _dot(p.astype(BF16), vv) / den
        o_ref[:, cols] = o.astype(o_ref.dtype)


def _attention(qkv, kb, vb, q_norm, rows0, seq, nb, tk, rope_tabs):
    tq = 256
    gw = GQA_GROUP * HEAD_DIM
    nq = seq // tq
    rb0 = rows0 // tq
    in_specs = [pl.BlockSpec((tq, gw), lambda b, h, i: (rb0 + b * nq + i, h)),
                pl.BlockSpec((tk, HEAD_DIM), lambda b, h, i: (b, h)),
                pl.BlockSpec((tk, HEAD_DIM), lambda b, h, i: (b, h)),
                pl.BlockSpec((1, HEAD_DIM), lambda b, h, i: (0, 0))]
    args = [qkv, kb, vb, q_norm.reshape(1, HEAD_DIM)]
    if rope_tabs is not None:
        in_specs += [pl.BlockSpec((tq, HEAD_DIM), lambda b, h, i: (i, 0))] * 2
        args += list(rope_tabs)
    vm = 2 * (tq * gw * 4 + 2 * tk * HEAD_DIM * 2 + tq * gw * 2) + 4 * tq * tk * 4 + (4 << 20)
    return pl.pallas_call(
        functools.partial(_attn_kernel, rope=rope_tabs is not None),
        grid=(nb, N_KV_HEADS, nq),
        in_specs=in_specs,
        out_specs=pl.BlockSpec((tq, gw), lambda b, h, i: (b * nq + i, h)),
        out_shape=jax.ShapeDtypeStruct((nb * seq, D_MODEL), BF16),
        compiler_params=_cparams(("parallel", "parallel", "parallel"), vm),
        name="attention",
    )(*args)


def _rope_tables():
    n_rows = DEC_SEQ // GRID_W
    row = jnp.repeat(jnp.arange(n_rows), GRID_W).astype(F32)
    colp = jnp.tile(jnp.arange(GRID_W), n_rows).astype(F32)
    inv = ROPE_THETA ** (-jnp.arange(0, ROPE_AXIS_DIM, 2, dtype=F32) / ROPE_AXIS_DIM)
    ang_r = row[:, None] * inv
    ang_c = colp[:, None] * inv
    ang = jnp.concatenate([ang_r, ang_r, ang_c, ang_c], axis=-1)
    quarter = ROPE_AXIS_DIM // 2
    sign = jnp.where((jnp.arange(HEAD_DIM) % ROPE_AXIS_DIM) < quarter, -1.0, 1.0).astype(F32)
    return jnp.cos(ang), jnp.sin(ang) * sign


def _attn_mixer(hbf, w_qkv, w_out, layer, q_norm, k_norm, cache_k, cache_v, x, gate):
    qkv = _matmul(hbf, w_qkv, layer, _MM_TM, _MM_TN)
    kw = N_KV_HEADS * HEAD_DIM
    tabs = _rope_tables()
    kn_ctx, v_ctx, kb_ctx, vb_ctx = _kv_prep(qkv, k_norm, 0, N_CTX, None)
    kb_lat, vb_lat = _kv_prep(qkv, k_norm, N_CTX, N_LAT, tabs)
    o_ctx = _attention(qkv, kb_ctx, vb_ctx, q_norm, 0, SEQ, BATCH, SEQ, None)
    tk = PAST_LEN + DEC_SEQ
    keys = jnp.concatenate([cache_k.reshape(DEC_BATCH, PAST_LEN, kw).astype(BF16),
                            kb_lat.reshape(DEC_BATCH, DEC_SEQ, kw)], axis=1).reshape(DEC_BATCH * tk, kw)
    vals = jnp.concatenate([cache_v.reshape(DEC_BATCH, PAST_LEN, kw).astype(BF16),
                            vb_lat.reshape(DEC_BATCH, DEC_SEQ, kw)], axis=1).reshape(DEC_BATCH * tk, kw)
    o_lat = _attention(qkv, keys, vals, q_norm, N_CTX, DEC_SEQ, DEC_BATCH, tk, tabs)
    o = jnp.concatenate([o_ctx, o_lat], axis=0)
    new_k = kn_ctx.reshape(BATCH, SEQ, N_KV_HEADS, HEAD_DIM)
    new_v = v_ctx.reshape(BATCH, SEQ, N_KV_HEADS, HEAD_DIM)
    return _matmul_residual(o, w_out, layer, x, gate, _MM_TM, _MM_TN), new_k, new_v


def kernel(x_prompt, x_sample, state_hgrn, cache_k, cache_v, c, c_ctx, ada_w, ada_b, norm_w, final_norm_w,
           hgrn_w_in, hgrn_lb_logits, hgrn_norm_w, hgrn_w_out, conv_w_in, conv_w, conv_w_out,
           attn_w_qkv, attn_q_norm, attn_k_norm, attn_w_out, ffn_w_gate_up, ffn_w_down,
           moe_w_router, moe_w_gate_up, moe_w_down):
    x = jnp.concatenate([x_prompt.reshape(N_CTX, D_MODEL), x_sample.reshape(N_LAT, D_MODEL)], axis=0)
    conds = jnp.zeros((COND_PAD, D_MODEL), F32).at[0].set(c_ctx).at[1:N_COND].set(c)
    mod = _modulation(conds, ada_w, ada_b)
    mod = mod.reshape(DEPTH, COND_PAD, 6, 1, D_MODEL).transpose(0, 2, 1, 3, 4)

    probs = jax.nn.softmax(hgrn_lb_logits.astype(F32), axis=0)
    csum = jnp.cumsum(probs, axis=0)
    lower_bounds = csum - csum[:1]

    ffn_gu, ffn_down = ffn_w_gate_up.astype(BF16), ffn_w_down.astype(BF16)
    moe_gu, moe_down = moe_w_gate_up.astype(BF16), moe_w_down.astype(BF16)
    hgrn_w_in, hgrn_w_out = hgrn_w_in.astype(BF16), hgrn_w_out.astype(BF16)
    conv_w_in, conv_w_out = conv_w_in.astype(BF16), conv_w_out.astype(BF16)
    attn_w_qkv, attn_w_out = attn_w_qkv.astype(BF16), attn_w_out.astype(BF16)

    new_states, new_k, new_v = [], None, None
    for layer in range(DEPTH):
        kind = layer % N_MIXERS
        j = layer // N_MIXERS
        sh1, sc1, g1, sh2, sc2, g2 = (mod[layer, t] for t in range(6))
        hbf = _norm_mod(x, norm_w[layer, 0], sc1, sh1)
        if kind == 0:
            x, st = _hgrn_mixer(hbf, hgrn_w_in, hgrn_w_out, j, lower_bounds[j], hgrn_norm_w[j],
                                state_hgrn[:, j], x, g1)
            new_states.append(st)
        elif kind == 1:
            x = _conv_mixer(hbf, conv_w_in, conv_w_out, j, conv_w[j], x, g1)
        else:
            x, new_k, new_v = _attn_mixer(hbf, attn_w_qkv, attn_w_out, j, attn_q_norm[j], attn_k_norm[j],
                                          cache_k[:, j], cache_v[:, j], x, g1)
        f = layer // 2
        if layer % 2 == 0:
            hbf = _norm_mod(x, norm_w[layer, 1], sc2, sh2)
            x = _dense_ffn(hbf, ffn_gu, ffn_down, f, x, g2)
        else:
            x = _moe_ffn(x, norm_w[layer, 1], sc2, sh2, g2, moe_w_router[f], moe_gu, moe_down, f)
    y_prompt = _final_norm(x, final_norm_w, 0, N_CTX).reshape(BATCH, SEQ, D_MODEL)
    y_sample = _final_norm(x, final_norm_w, N_CTX, N_LAT).reshape(DEC_BATCH, DEC_SEQ, D_MODEL)
    new_state_hgrn = jnp.stack(new_states, axis=1)
    return (y_prompt, y_sample, new_state_hgrn, new_k[:, None], new_v[:, None])
```

```python
import functools
import math

import numpy as np
import jax
import jax.numpy as jnp
from jax import lax
from jax.experimental import pallas as pl
from jax.experimental.pallas import tpu as pltpu

F32 = jnp.float32
BF16 = jnp.bfloat16

D_MODEL = 2048
BATCH = 32
SEQ = 256
DEPTH = 4
DEC_BATCH = 8
DEC_SEQ = 2048
PAST_LEN = 512
GRID_W = 64
N_MIXERS = 3
EPS = 1e-6
HG_HEADS = 16
HG_DK = 128
HG_DV = 128
HG_FDIM = HG_HEADS * HG_DK
CONV_W = 3
HEAD_DIM = 128
N_HEADS = 16
N_KV_HEADS = 4
GQA_GROUP = 4
ROPE_THETA = 10000.0
ROPE_AXIS_DIM = HEAD_DIM // 2
D_FF = 5632
N_EXPERTS = 8
TOP_K = 2

N_CTX = BATCH * SEQ
N_LAT = DEC_BATCH * DEC_SEQ
N_TOK = N_CTX + N_LAT
N_COND = 1 + DEC_BATCH
COND_PAD = 16

V7X_LANES = 128
LANE_CHUNKS = D_MODEL // V7X_LANES
V7X_VMEM_BYTES = 64 * 1024 * 1024
VMEM_CAP = 56 * 1024 * 1024

HG_CHUNK = 64
HG_LEVELS = int(math.log2(HG_CHUNK))
HG_TOT_ROWS = 16
HG_SPLIT = 2
HG_PAIR_W = 2 * HG_DK
HG_TB = 256


def _cparams(sem, vmem_bytes):
    return pltpu.CompilerParams(dimension_semantics=sem,
                                vmem_limit_bytes=int(min(VMEM_CAP, max(vmem_bytes, 16 * 1024 * 1024))))


def _cond_of_tile(i, tm):
    r = i * tm
    return jnp.where(r < N_CTX, 0, 1 + (r - N_CTX) // DEC_SEQ)


def _silu(x):
    return x * jax.nn.sigmoid(x)


def _dot(a, b):
    return jnp.dot(a, b, preferred_element_type=F32)


def _dot_nt(a, b):
    return lax.dot_general(a, b, (((1,), (1,)), ((), ())), preferred_element_type=F32)


def _dot_tn(a, b):
    return lax.dot_general(a, b, (((0,), (0,)), ((), ())), preferred_element_type=F32)


def _mod_kernel(c_ref, w_ref, b_ref, o_ref):
    a = _silu(c_ref[...]).astype(BF16)
    o_ref[...] = _dot(a, w_ref[...].astype(BF16)) + b_ref[...]


def _modulation(conds, ada_w, ada_b):
    tn = 1024
    n_out = 6 * D_MODEL
    return pl.pallas_call(
        _mod_kernel,
        grid=(DEPTH, n_out // tn),
        in_specs=[pl.BlockSpec((COND_PAD, D_MODEL), lambda l, j: (0, 0)),
                  pl.BlockSpec((None, D_MODEL, tn), lambda l, j: (l, 0, j)),
                  pl.BlockSpec((None, 1, tn), lambda l, j: (l, 0, j))],
        out_specs=pl.BlockSpec((None, COND_PAD, tn), lambda l, j: (l, 0, j)),
        out_shape=jax.ShapeDtypeStruct((DEPTH, COND_PAD, n_out), F32),
        compiler_params=_cparams(("parallel", "parallel"), 3 * D_MODEL * tn * 4 + (4 << 20)),
        name="modulation",
    )(conds, ada_w, ada_b.reshape(DEPTH, 1, n_out))


def _norm_body(x_ref, w_ref, sc_ref, sh_ref):
    x = x_ref[...]
    y = x * lax.rsqrt(jnp.mean(x * x, axis=-1, keepdims=True) + EPS) * w_ref[...]
    return y * (1.0 + sc_ref[...]) + sh_ref[...]


def _norm_mod_kernel(x_ref, w_ref, sc_ref, sh_ref, o_ref):
    o_ref[...] = _norm_body(x_ref, w_ref, sc_ref, sh_ref).astype(o_ref.dtype)


def _final_norm_kernel(x_ref, w_ref, o_ref):
    x = x_ref[...]
    o_ref[...] = x * lax.rsqrt(jnp.mean(x * x, axis=-1, keepdims=True) + EPS) * w_ref[...]


def _norm_router_kernel(x_ref, w_ref, sc_ref, sh_ref, rh_ref, rl_ref, h_ref, r_ref):
    h = _norm_body(x_ref, w_ref, sc_ref, sh_ref)
    tm = h.shape[0]
    for cc in range(LANE_CHUNKS):
        h_ref[pl.ds(cc, tm, stride=LANE_CHUNKS), :] = h[:, cc * V7X_LANES:(cc + 1) * V7X_LANES]
    hh = h.astype(BF16)
    hl = (h - hh.astype(F32)).astype(BF16)
    logits = _dot(hh, rh_ref[...]) + (_dot(hh, rl_ref[...]) + _dot(hl, rh_ref[...]))
    lane = lax.broadcasted_iota(jnp.int32, logits.shape, 1)
    neg = jnp.float32(-jnp.inf)
    logits = jnp.where(lane < N_EXPERTS, logits, neg)
    v1 = jnp.max(logits, axis=-1, keepdims=True)
    i1 = jnp.min(jnp.where(logits == v1, lane, V7X_LANES), axis=-1, keepdims=True)
    rest = jnp.where(lane == i1, neg, logits)
    v2 = jnp.max(rest, axis=-1, keepdims=True)
    i2 = jnp.min(jnp.where(rest == v2, lane, V7X_LANES), axis=-1, keepdims=True)
    e = jnp.exp(v2 - v1)
    g1 = 1.0 / (1.0 + e)
    g2 = e * g1
    out = jnp.where(lane == 0, i1.astype(F32), 0.0)
    out = jnp.where(lane == 1, i2.astype(F32), out)
    out = jnp.where(lane == 2, g1, out)
    out = jnp.where(lane == 3, g2, out)
    r_ref[...] = out


_NORM_TM = 512


def _row_specs(tm):
    x_spec = pl.BlockSpec((tm, D_MODEL), lambda i: (i, 0))
    w_spec = pl.BlockSpec((1, D_MODEL), lambda i: (0, 0))
    c_spec = pl.BlockSpec((None, 1, D_MODEL), lambda i: (_cond_of_tile(i, tm), 0, 0))
    return x_spec, w_spec, c_spec


def _norm_mod(x, w, sc, sh, out_dtype=BF16):
    tm = _NORM_TM
    x_spec, w_spec, c_spec = _row_specs(tm)
    return pl.pallas_call(
        _norm_mod_kernel,
        grid=(N_TOK // tm,),
        in_specs=[x_spec, w_spec, c_spec, c_spec],
        out_specs=x_spec,
        out_shape=jax.ShapeDtypeStruct((N_TOK, D_MODEL), out_dtype),
        compiler_params=_cparams(("parallel",), 6 * tm * D_MODEL * 4),
        name="norm_mod",
    )(x, w.reshape(1, D_MODEL), sc, sh)


def _final_norm(x, w, rows0, nrows):
    tm = _NORM_TM
    rb0 = rows0 // tm
    return pl.pallas_call(
        _final_norm_kernel,
        grid=(nrows // tm,),
        in_specs=[pl.BlockSpec((tm, D_MODEL), lambda i: (rb0 + i, 0)),
                  pl.BlockSpec((1, D_MODEL), lambda i: (0, 0))],
        out_specs=pl.BlockSpec((tm, D_MODEL), lambda i: (i, 0)),
        out_shape=jax.ShapeDtypeStruct((nrows, D_MODEL), F32),
        compiler_params=_cparams(("parallel",), 6 * tm * D_MODEL * 4),
        name="final_norm",
    )(x, w.reshape(1, D_MODEL))


def _norm_router(x, w, sc, sh, w_router):
    tm = _NORM_TM
    x_spec, w_spec, c_spec = _row_specs(tm)
    wr = jnp.zeros((D_MODEL, V7X_LANES), F32).at[:, :N_EXPERTS].set(w_router)
    wr_hi = wr.astype(BF16)
    wr_lo = (wr - wr_hi.astype(F32)).astype(BF16)
    r_spec = pl.BlockSpec((D_MODEL, V7X_LANES), lambda i: (0, 0))
    return pl.pallas_call(
        _norm_router_kernel,
        grid=(N_TOK // tm,),
        in_specs=[x_spec, w_spec, c_spec, c_spec, r_spec, r_spec],
        out_specs=[pl.BlockSpec((tm * LANE_CHUNKS, V7X_LANES), lambda i: (i, 0)),
                   pl.BlockSpec((tm, V7X_LANES), lambda i: (i, 0))],
        out_shape=[jax.ShapeDtypeStruct((N_TOK * LANE_CHUNKS, V7X_LANES), F32),
                   jax.ShapeDtypeStruct((N_TOK, V7X_LANES), F32)],
        compiler_params=_cparams(("parallel",), 8 * tm * D_MODEL * 4),
        name="norm_router",
    )(x, w.reshape(1, D_MODEL), sc, sh, wr_hi, wr_lo)


_MM_TM = 1024
_MM_TN = 1024


def _mm_kernel(a_ref, w_ref, o_ref):
    o_ref[...] = _dot(a_ref[...], w_ref[...]).astype(o_ref.dtype)


def _mm_res_kernel(a_ref, w_ref, x_ref, g_ref, o_ref):
    o_ref[...] = x_ref[...] + g_ref[...] * _dot(a_ref[...], w_ref[...])


def _matmul(a, w_all, layer, tm, tn, out_dtype=F32):
    m, k = a.shape
    n = w_all.shape[2]
    wb = w_all.dtype.itemsize
    vm = 2 * (tm * k * 2 + k * tn * wb + tm * tn * 4) + tm * tn * 4 + k * tn * 2
    return pl.pallas_call(
        _mm_kernel,
        grid=(m // tm, n // tn),
        in_specs=[pl.BlockSpec((tm, k), lambda i, j: (i, 0)),
                  pl.BlockSpec((None, k, tn), lambda i, j: (layer, 0, j))],
        out_specs=pl.BlockSpec((tm, tn), lambda i, j: (i, j)),
        out_shape=jax.ShapeDtypeStruct((m, n), out_dtype),
        compiler_params=_cparams(("parallel", "parallel"), vm + (4 << 20)),
        name="matmul",
    )(a, w_all)


def _matmul_residual(a, w_all, layer, x, gate, tm, tn):
    m, k = a.shape
    n = w_all.shape[2]
    wb = w_all.dtype.itemsize
    vm = 2 * (tm * k * 2 + k * tn * wb + 2 * tm * tn * 4) + tm * tn * 4 + k * tn * 2
    return pl.pallas_call(
        _mm_res_kernel,
        grid=(m // tm, n // tn),
        in_specs=[pl.BlockSpec((tm, k), lambda i, j: (i, 0)),
                  pl.BlockSpec((None, k, tn), lambda i, j: (layer, 0, j)),
                  pl.BlockSpec((tm, tn), lambda i, j: (i, j)),
                  pl.BlockSpec((None, 1, tn), lambda i, j: (_cond_of_tile(i, tm), 0, j))],
        out_specs=pl.BlockSpec((tm, tn), lambda i, j: (i, j)),
        out_shape=jax.ShapeDtypeStruct((m, n), F32),
        compiler_params=_cparams(("parallel", "parallel"), vm + (4 << 20)),
        name="matmul_residual",
    )(a, w_all, x, gate)


_FFN_TF = 512


def _ffn_partial(x_ref, wa_ref, wg_ref, wd_ref):
    x = x_ref[...]
    a = _dot(x, wa_ref[...])
    g = _dot(x, wg_ref[...])
    hid = (_silu(a) * g).astype(BF16)
    return _dot(hid, wd_ref[...])


def _ffn_moe_kernel(te_ref, nu_ref, rt_ref, h_hbm, wa_ref, wg_ref, wd_ref, o_ref, stage_a, stage_b, xb, sem):
    i = pl.program_id(0)
    f = pl.program_id(1)
    nt = pl.num_programs(0)
    nf = pl.num_programs(1)
    tm = o_ref.shape[0]
    lc = LANE_CHUNKS

    def issue(tile, dst_buf, dst_sem, r):
        tok = rt_ref[tile * tm + r]
        src = h_hbm.at[pl.ds(pl.multiple_of(tok * lc, lc), lc), :]
        pltpu.make_async_copy(src, dst_buf.at[pl.ds(r * lc, lc), :], sem.at[dst_sem]).start()

    def wait(buf, on_sem):
        pltpu.make_async_copy(h_hbm.at[pl.ds(0, tm * lc), :], buf, sem.at[on_sem]).wait()

    @pl.when((i == 0) & (f == 0))
    def _():
        def body(r, carry):
            issue(0, stage_a, 0, r)
            return carry
        lax.fori_loop(0, tm, body, 0, unroll=_DMA_ISSUE_UNROLL)

    nxt = jnp.minimum(i + 1, nt - 1)
    for parity, (cur, other) in enumerate(((stage_a, stage_b), (stage_b, stage_a))):
        @pl.when((f == 0) & (i % 2 == parity))
        def _(parity=parity, cur=cur, other=other):
            wait(cur, parity)
            for cc in range(lc):
                for r in range(cc * tm // lc, (cc + 1) * tm // lc):
                    issue(nxt, other, 1 - parity, r)
                xb[:, cc * V7X_LANES:(cc + 1) * V7X_LANES] = cur[pl.ds(cc, tm, stride=lc), :].astype(BF16)
            o_ref[...] = jnp.zeros(o_ref.shape, o_ref.dtype)

    @pl.when(i < nu_ref[0])
    def _():
        o_ref[...] += _ffn_partial(xb, wa_ref, wg_ref, wd_ref)

    for parity, other in enumerate((stage_b, stage_a)):
        @pl.when((i == nt - 1) & (f == nf - 1) & (i % 2 == parity))
        def _(parity=parity, other=other):
            wait(other, 1 - parity)


def _moe_ffn_call(h, row_tok, w_gu, w_down, layer, tile_expert, n_used):
    tm = _MOE_TM
    tf = _FFN_TF
    nf = D_FF // tf

    def hid_tile(i, f, nu):
        return jnp.where(i < nu[0], f, 0)

    vm = (2 * tm * D_MODEL * 4 + tm * D_MODEL * 2 + 2 * (3 * D_MODEL * tf * 2 + tm * D_MODEL * 4)
          + tm * D_MODEL * 4 + 3 * tm * tf * 4)
    return pl.pallas_call(
        _ffn_moe_kernel,
        grid_spec=pltpu.PrefetchScalarGridSpec(
            num_scalar_prefetch=3,
            grid=(_MOE_ROWS // tm, nf),
            in_specs=[pl.BlockSpec(memory_space=pl.ANY),
                      pl.BlockSpec((None, None, D_MODEL, tf),
                                   lambda i, f, te, nu, rt: (layer, te[i], 0, hid_tile(i, f, nu))),
                      pl.BlockSpec((None, None, D_MODEL, tf),
                                   lambda i, f, te, nu, rt: (layer, te[i], 0, hid_tile(i, f, nu) + nf)),
                      pl.BlockSpec((None, None, tf, D_MODEL),
                                   lambda i, f, te, nu, rt: (layer, te[i], hid_tile(i, f, nu), 0))],
            out_specs=pl.BlockSpec((tm, D_MODEL), lambda i, f, te, nu, rt: (i, 0)),
            scratch_shapes=[pltpu.VMEM((tm * LANE_CHUNKS, V7X_LANES), F32),
                            pltpu.VMEM((tm * LANE_CHUNKS, V7X_LANES), F32),
                            pltpu.VMEM((tm, D_MODEL), BF16),
                            pltpu.SemaphoreType.DMA((2,))]),
        out_shape=jax.ShapeDtypeStruct((_MOE_ROWS, D_MODEL), F32),
        compiler_params=_cparams(("arbitrary", "arbitrary"), vm + (4 << 20)),
        name="moe_swiglu",
    )(tile_expert, n_used, row_tok, h, w_gu, w_gu, w_down)


def _ffn_res_kernel(x_ref, wa_ref, wg_ref, wd_ref, r_ref, g_ref, o_ref):
    @pl.when(pl.program_id(1) == 0)
    def _():
        o_ref[...] = r_ref[...]

    o_ref[...] += g_ref[...] * _ffn_partial(x_ref, wa_ref, wg_ref, wd_ref)


def _dense_ffn(hb, w_gu, w_down, layer, x, gate):
    tm = 512
    tf = _FFN_TF
    nf = D_FF // tf
    vm = 2 * (tm * D_MODEL * 2 + 3 * D_MODEL * tf * 2 + 2 * tm * D_MODEL * 4) + tm * D_MODEL * 4 + 3 * tm * tf * 4
    return pl.pallas_call(
        _ffn_res_kernel,
        grid=(N_TOK // tm, nf),
        in_specs=[pl.BlockSpec((tm, D_MODEL), lambda i, f: (i, 0)),
                  pl.BlockSpec((None, D_MODEL, tf), lambda i, f: (layer, 0, f)),
                  pl.BlockSpec((None, D_MODEL, tf), lambda i, f: (layer, 0, f + nf)),
                  pl.BlockSpec((None, tf, D_MODEL), lambda i, f: (layer, f, 0)),
                  pl.BlockSpec((tm, D_MODEL), lambda i, f: (i, 0)),
                  pl.BlockSpec((None, 1, D_MODEL), lambda i, f: (_cond_of_tile(i, tm), 0, 0))],
        out_specs=pl.BlockSpec((tm, D_MODEL), lambda i, f: (i, 0)),
        out_shape=jax.ShapeDtypeStruct((N_TOK, D_MODEL), F32),
        compiler_params=_cparams(("parallel", "arbitrary"), vm + (4 << 20)),
        name="swiglu",
    )(hb, w_gu, w_gu, w_down, x, gate)


_MOE_TM = 512
_MOE_ROWS = TOP_K * N_TOK + N_EXPERTS * _MOE_TM
_GATHER_TM = 256
_DMA_ISSUE_UNROLL = 8


def _combine_kernel(d0_ref, d1_ref, y_hbm, x_ref, gt_ref, w_ref, o_ref, buf, sem):
    i = pl.program_id(0)
    tm = o_ref.shape[0]

    def issue(tile, slot):
        def body(r, carry):
            t = tile * tm + r
            pltpu.make_async_copy(y_hbm.at[pl.ds(d0_ref[t], 1), :], buf.at[slot, 0, pl.ds(r, 1), :],
                                  sem.at[slot]).start()
            pltpu.make_async_copy(y_hbm.at[pl.ds(d1_ref[t], 1), :], buf.at[slot, 1, pl.ds(r, 1), :],
                                  sem.at[slot]).start()
            return carry
        lax.fori_loop(0, tm, body, 0, unroll=_DMA_ISSUE_UNROLL)

    @pl.when(i == 0)
    def _():
        issue(0, 0)

    @pl.when(i + 1 < pl.num_programs(0))
    def _():
        issue(i + 1, (i + 1) % 2)

    slot = i % 2
    pltpu.make_async_copy(y_hbm.at[pl.ds(0, tm), :], buf.at[slot, 0], sem.at[slot]).wait()
    pltpu.make_async_copy(y_hbm.at[pl.ds(0, tm), :], buf.at[slot, 1], sem.at[slot]).wait()
    w = w_ref[...]
    y = w[:, 2:3] * buf[slot, 0] + w[:, 3:4] * buf[slot, 1]
    o_ref[...] = x_ref[...] + gt_ref[...] * y


def _combine_rows(y, d0, d1, x, gate, route):
    tm = _GATHER_TM
    return pl.pallas_call(
        _combine_kernel,
        grid_spec=pltpu.PrefetchScalarGridSpec(
            num_scalar_prefetch=2,
            grid=(N_TOK // tm,),
            in_specs=[pl.BlockSpec(memory_space=pl.ANY),
                      pl.BlockSpec((tm, D_MODEL), lambda i, a, b: (i, 0)),
                      pl.BlockSpec((None, 1, D_MODEL), lambda i, a, b: (_cond_of_tile(i, tm), 0, 0)),
                      pl.BlockSpec((tm, V7X_LANES), lambda i, a, b: (i, 0))],
            out_specs=pl.BlockSpec((tm, D_MODEL), lambda i, a, b: (i, 0)),
            scratch_shapes=[pltpu.VMEM((2, 2, tm, D_MODEL), F32), pltpu.SemaphoreType.DMA((2,))]),
        out_shape=jax.ShapeDtypeStruct((N_TOK, D_MODEL), F32),
        compiler_params=_cparams(("arbitrary",), 10 * tm * D_MODEL * 4),
        name="moe_combine",
    )(d0, d1, y, x, gate, route)


def _moe_ffn(x, norm_w, sc, sh, gate, w_router, w_gu, w_down, layer):
    h, route = _norm_router(x, norm_w, sc, sh, w_router)
    top_i = route[:, :TOP_K].astype(jnp.int32)
    member = jnp.sum(top_i[:, :, None] == jnp.arange(N_EXPERTS, dtype=jnp.int32), axis=1, dtype=jnp.int32)
    before = jnp.cumsum(member, axis=0) - member
    counts = before[-1] + member[-1]
    padded = ((counts + _MOE_TM - 1) // _MOE_TM) * _MOE_TM
    ends = jnp.cumsum(padded)
    starts = ends - padded
    dest = jnp.take(starts, top_i) + jnp.take_along_axis(before, top_i, axis=1)
    tok = jnp.broadcast_to(jnp.arange(N_TOK, dtype=jnp.int32)[:, None], (N_TOK, TOP_K))
    row_tok = jnp.zeros((_MOE_ROWS,), jnp.int32).at[dest.reshape(-1)].set(tok.reshape(-1))
    n_tiles = _MOE_ROWS // _MOE_TM
    tile_start = jnp.arange(n_tiles, dtype=jnp.int32) * _MOE_TM
    tile_expert = jnp.minimum(jnp.sum(tile_start[:, None] >= ends[None, :], axis=1), N_EXPERTS - 1).astype(jnp.int32)
    n_used = (ends[-1:] // _MOE_TM).astype(jnp.int32)
    ys = _moe_ffn_call(h, row_tok, w_gu, w_down, layer, tile_expert, n_used)
    return _combine_rows(ys, dest[:, 0], dest[:, 1], x, gate, route)


@functools.lru_cache(maxsize=None)
def _hgrn_tables():
    c = HG_CHUNK
    n_rows = c * (HG_LEVELS + 1) + HG_TOT_ROWS
    prefix = np.zeros((2, n_rows, c), np.float32)
    role = np.zeros((2, HG_LEVELS, c, 1), np.float32)
    pair = np.zeros((2, HG_LEVELS, c, c), np.float32)
    for d in range(2):
        for i in range(c):
            if d == 0:
                prefix[d, i, :i + 1] = 1.0
            else:
                prefix[d, i, i:] = 1.0
        for l in range(HG_LEVELS):
            m = 2 ** l
            for i in range(c):
                blk = i // (2 * m)
                upper = (i // m) % 2 == 1
                mid = blk * 2 * m + m
                row = c * (l + 1) + i
                if d == 0:
                    is_q = upper
                    lo, hi = (mid, i + 1) if upper else (i + 1, mid)
                else:
                    is_q = not upper
                    lo, hi = (mid, i) if upper else (i, mid)
                prefix[d, row, lo:hi] = 1.0
                role[d, l, i, 0] = 1.0 if is_q else 0.0
            for i in range(c):
                for j in range(c):
                    same = i // (2 * m) == j // (2 * m)
                    if same and role[d, l, i, 0] == 1.0 and role[d, l, j, 0] == 0.0:
                        pair[d, l, i, j] = 1.0
        prefix[d, c * (HG_LEVELS + 1):, :] = 1.0
    prefix = np.concatenate([prefix] * HG_SPLIT, axis=2)
    pair = np.concatenate([pair, pair], axis=3)
    return prefix, pair


def _pair_blockdiag(x2):
    zero = jnp.zeros((x2.shape[0], HG_DK), x2.dtype)
    return jnp.concatenate([jnp.concatenate([x2[:, :HG_DK], zero], axis=1),
                            jnp.concatenate([zero, x2[:, HG_DK:]], axis=1)], axis=0)


def _role_select(qh, k, level, backward):
    m = 2 ** level
    if m % 8 == 0:
        parts = []
        for r in range(0, HG_CHUNK, m):
            upper = (r // m) % 2 == 1
            parts.append((qh if upper != backward else k)[r:r + m])
        return jnp.concatenate(parts, axis=0)
    row = lax.broadcasted_iota(jnp.int32, qh.shape, 0)
    upper = (row // m) % 2 == 1
    return jnp.where(upper != backward, qh, k)


def _hgrn_dir_kernel(*refs, backward, hb, has_init, write_state):
    it = iter(refs)
    q_ref, f_ref, v_ref, la_ref, l1_ref, pm_ref, pair_ref = (next(it) for _ in range(7))
    s0_ref = next(it) if has_init else None
    if backward:
        of_ref, gt_ref, nw_ref = next(it), next(it), next(it)
    o_ref = next(it)
    st_ref = next(it) if write_state else None
    st_scr = next(it)

    c = HG_CHUNK
    n_chunk = HG_TB // c
    n_pair = hb // 2
    t = pl.program_id(2)

    @pl.when(t == 0)
    def _():
        for p in range(n_pair):
            if has_init:
                st_scr[p] = jnp.concatenate([s0_ref[2 * p].T, s0_ref[2 * p + 1].T], axis=1)
            else:
                st_scr[p] = jnp.zeros((HG_DV, HG_PAIR_W), F32)

    def pair_unit(p, ci):
        r0 = ci * c
        cols = slice(p * HG_PAIR_W, (p + 1) * HG_PAIR_W)
        qh = _silu(q_ref[pl.ds(r0, c), cols])
        z = f_ref[pl.ds(r0, c), cols]
        v = v_ref[pl.ds(r0, c), cols]
        la = la_ref[:, cols]
        b = l1_ref[:, cols] + (jnp.minimum(z, 0.0) - jnp.log(1.0 + jnp.exp(-jnp.abs(z))))
        g = jnp.maximum(la, b) + jnp.log(1.0 + jnp.exp(-jnp.abs(la - b)))
        g = g * math.log2(math.e)
        k = 1.0 - jnp.exp2(g)
        g1 = g.astype(BF16)
        g2 = (g - g1.astype(F32)).astype(BF16)
        sums = _dot(pm_ref[...], jnp.concatenate([g1, g2], axis=0))
        cum = sums[0:c]
        tot = sums[c * (HG_LEVELS + 1):c * (HG_LEVELS + 1) + 1]
        scores = jnp.zeros((c, HG_PAIR_W // 2), F32)
        for l in range(HG_LEVELS):
            e = jnp.exp2(sums[c * (l + 1):c * (l + 2)])
            xk = (_role_select(qh, k, l, backward) * e).astype(BF16)
            scores = scores + pair_ref[l] * _dot_nt(xk, _pair_blockdiag(xk))
        vb = v.astype(BF16)
        qk = qh * k
        lane = lax.broadcasted_iota(jnp.int32, qk.shape, 1)
        self_score = jnp.where(lane < HG_DK, jnp.sum(qk[:, :HG_DK], axis=-1, keepdims=True),
                               jnp.sum(qk[:, HG_DK:], axis=-1, keepdims=True))
        o = _dot(scores.astype(BF16), _pair_blockdiag(vb)) + self_score * v
        st = st_scr[p]
        o = o + _dot_nt((qh * jnp.exp2(cum)).astype(BF16), _pair_blockdiag(st.astype(BF16)))
        kt = (k * jnp.exp2(tot - cum)).astype(BF16)
        v_rows = jnp.concatenate([vb[:, :HG_DV], vb[:, HG_DV:]], axis=0)
        st_scr[p] = st * jnp.exp2(tot) + _dot_tn(v_rows, _pair_blockdiag(kt))
        if backward:
            o = o + of_ref[pl.ds(r0, c), cols]
            gate = _silu(gt_ref[pl.ds(r0, c), cols])
            for hh in range(2):
                hc = slice(hh * HG_DV, (hh + 1) * HG_DV)
                oh = o[:, hc]
                y = oh * lax.rsqrt(jnp.mean(oh * oh, axis=-1, keepdims=True) + EPS) * nw_ref[...]
                o_ref[pl.ds(r0, c), pl.ds(p * HG_PAIR_W + hh * HG_DV, HG_DV)] = (y * gate[:, hc]).astype(o_ref.dtype)
        else:
            o_ref[pl.ds(r0, c), cols] = o

    for i in range(n_chunk):
        ci = n_chunk - 1 - i if backward else i
        for p in range(n_pair):
            pair_unit(p, ci)

    if write_state:
        @pl.when(t == pl.num_programs(2) - 1)
        def _():
            for p in range(n_pair):
                st = st_scr[p]
                st_ref[2 * p] = st[:, :HG_DK].T
                st_ref[2 * p + 1] = st[:, HG_DK:].T


def _hgrn_scan(p, log_lb, log1m_lb, norm_w, rows0, seq, nb, hb, s0=None, write_state=False):
    w = hb * HG_DK
    ngrp = HG_HEADS // hb
    per = D_MODEL // w
    tb = HG_TB
    n_t = seq // tb
    rb0 = rows0 // tb
    prefix, pair = _hgrn_tables()
    outs = []
    o_fwd = None
    for d in range(2):
        def tblk(t, d=d):
            return n_t - 1 - t if d == 1 else t

        def col(group, d=d, tblk=tblk):
            return pl.BlockSpec((tb, w), lambda b, g, t: (rb0 + b * n_t + tblk(t), group * per + g))

        def row(d=d, tblk=tblk):
            return pl.BlockSpec((tb, w), lambda b, g, t: (b * n_t + tblk(t), g))

        in_specs = [col(0), col(1 + d), col(3),
                    pl.BlockSpec((None, 1, w), lambda b, g, t, d=d: (d, 0, g)),
                    pl.BlockSpec((None, 1, w), lambda b, g, t, d=d: (d, 0, g)),
                    pl.BlockSpec(prefix.shape[1:], lambda b, g, t: (0, 0)),
                    pl.BlockSpec(pair.shape[1:], lambda b, g, t: (0, 0, 0))]
        args = [p, p, p, log_lb, log1m_lb, jnp.asarray(prefix[d], BF16), jnp.asarray(pair[d])]
        if s0 is not None:
            in_specs.append(pl.BlockSpec((None, None, hb, HG_DK, HG_DV), lambda b, g, t, d=d: (b, d, g, 0, 0)))
            args.append(s0)
        if d == 1:
            in_specs += [row(), col(4), pl.BlockSpec((1, HG_DV), lambda b, g, t: (0, 0))]
            args += [o_fwd, p, norm_w.reshape(1, HG_DV)]
        out_specs = [row()]
        out_shape = [jax.ShapeDtypeStruct((nb * seq, D_MODEL), BF16 if d == 1 else F32)]
        if write_state:
            out_specs.append(pl.BlockSpec((None, hb, HG_DK, HG_DV), lambda b, g, t: (b, g, 0, 0)))
            out_shape.append(jax.ShapeDtypeStruct((nb, HG_HEADS, HG_DK, HG_DV), F32))
        vm = 2 * 7 * tb * w * 4 + 6 * hb * HG_DK * HG_DV * 4 + (16 << 20)
        res = pl.pallas_call(
            functools.partial(_hgrn_dir_kernel, backward=d == 1, hb=hb, has_init=s0 is not None,
                              write_state=write_state),
            grid=(nb, ngrp, n_t),
            in_specs=in_specs,
            out_specs=out_specs,
            out_shape=out_shape,
            scratch_shapes=[pltpu.VMEM((hb // 2, HG_DV, HG_PAIR_W), F32)],
            compiler_params=_cparams(("parallel", "parallel", "arbitrary"), vm),
            name="hgrn_bwd" if d == 1 else "hgrn_fwd",
        )(*args)
        if d == 0:
            o_fwd = res[0]
        outs.append(res)
    o = outs[1][0]
    if write_state:
        return o, jnp.stack([outs[0][1], outs[1][1]], axis=1)
    return (o,)


def _hgrn_mixer(hbf, w_in, w_out, layer, lb, norm_w, s0_lat, x, gate):
    p = _matmul(hbf, w_in, layer, _MM_TM, _MM_TN)
    log_lb = jnp.log(lb).reshape(2, 1, HG_FDIM)
    log1m_lb = jnp.log1p(-lb).reshape(2, 1, HG_FDIM)
    o_ctx, st = _hgrn_scan(p, log_lb, log1m_lb, norm_w, 0, SEQ, BATCH, 16, write_state=True)
    (o_lat,) = _hgrn_scan(p, log_lb, log1m_lb, norm_w, N_CTX, DEC_SEQ, DEC_BATCH, 16, s0=s0_lat)
    o = jnp.concatenate([o_ctx, o_lat], axis=0)
    return _matmul_residual(o, w_out, layer, x, gate, _MM_TM, _MM_TN), st


_CONV_TM = DEC_SEQ


def _conv_mm_kernel(a_ref, wb_ref, wc_ref, wx_ref, cw_ref, o_ref):
    a = a_ref[...]
    b_gate = _dot(a, wb_ref[...])
    u = _dot(a, wc_ref[...]) * _dot(a, wx_ref[...])
    tm = u.shape[0]
    seq = jnp.where(pl.program_id(0) * tm < N_CTX, SEQ, DEC_SEQ)
    pos = lax.broadcasted_iota(jnp.int32, u.shape, 0) & (seq - 1)
    prev = jnp.where(pos == 0, 0.0, pltpu.roll(u, 1, 0))
    nxt = jnp.where(pos == seq - 1, 0.0, pltpu.roll(u, tm - 1, 0))
    w = cw_ref[...]
    y = prev * w[0:1] + u * w[1:2] + nxt * w[2:3]
    o_ref[...] = (b_gate * y).astype(o_ref.dtype)


def _conv_in(hbf, w_in, layer, conv_w):
    assert SEQ & (SEQ - 1) == 0 and DEC_SEQ & (DEC_SEQ - 1) == 0 and N_CTX % _CONV_TM == 0
    tm = _CONV_TM
    tn = 256
    per = D_MODEL // tn

    def wcol(group):
        return pl.BlockSpec((None, D_MODEL, tn), lambda i, j: (layer, 0, group * per + j))

    vm = 2 * (tm * D_MODEL * 2 + 3 * D_MODEL * tn * 2 + tm * tn * 2) + 8 * tm * tn * 4
    return pl.pallas_call(
        _conv_mm_kernel,
        grid=(N_TOK // tm, per),
        in_specs=[pl.BlockSpec((tm, D_MODEL), lambda i, j: (i, 0)), wcol(0), wcol(1), wcol(2),
                  pl.BlockSpec((CONV_W, tn), lambda i, j: (0, j))],
        out_specs=pl.BlockSpec((tm, tn), lambda i, j: (i, j)),
        out_shape=jax.ShapeDtypeStruct((N_TOK, D_MODEL), BF16),
        compiler_params=_cparams(("parallel", "parallel"), vm + (4 << 20)),
        name="conv_in",
    )(hbf, w_in, w_in, w_in, conv_w)


def _conv_mixer(hbf, w_in, w_out, layer, conv_w, x, gate):
    o = _conv_in(hbf, w_in, layer, conv_w)
    return _matmul_residual(o, w_out, layer, x, gate, _MM_TM, _MM_TN)


def _head_norm(x, w):
    return x * lax.rsqrt(jnp.mean(x * x, axis=-1, keepdims=True) + EPS) * w


def _rope(x, cos, sin_signed):
    lane = lax.broadcasted_iota(jnp.int32, x.shape, 1)
    first = (lane % ROPE_AXIS_DIM) < (ROPE_AXIS_DIM // 2)
    rot = jnp.where(first, pltpu.roll(x, HEAD_DIM - ROPE_AXIS_DIM // 2, 1), pltpu.roll(x, ROPE_AXIS_DIM // 2, 1))
    return x * cos + rot * sin_signed


def _kvprep_kernel(*refs, rope):
    if rope:
        k_ref, v_ref, w_ref, cos_ref, sin_ref, kb_ref, vb_ref = refs
    else:
        k_ref, v_ref, w_ref, kn_ref, vf_ref, kb_ref, vb_ref = refs
        vf_ref[...] = v_ref[...]
    vb_ref[...] = v_ref[...].astype(BF16)
    for h in range(N_KV_HEADS):
        cols = slice(h * HEAD_DIM, (h + 1) * HEAD_DIM)
        kn = _head_norm(k_ref[:, cols], w_ref[...])
        if rope:
            kn = _rope(kn, cos_ref[...], sin_ref[...])
        else:
            kn_ref[:, cols] = kn
        kb_ref[:, cols] = kn.astype(BF16)


def _kv_prep(qkv, k_norm, rows0, nrows, rope_tabs):
    tm = 512
    kw = N_KV_HEADS * HEAD_DIM
    kcol = (N_HEADS * HEAD_DIM) // kw
    rb0 = rows0 // tm
    in_specs = [pl.BlockSpec((tm, kw), lambda i: (rb0 + i, kcol)),
                pl.BlockSpec((tm, kw), lambda i: (rb0 + i, kcol + 1)),
                pl.BlockSpec((1, HEAD_DIM), lambda i: (0, 0))]
    args = [qkv, qkv, k_norm.reshape(1, HEAD_DIM)]
    o_spec = pl.BlockSpec((tm, kw), lambda i: (i, 0))
    out_specs = [o_spec, o_spec]
    out_shape = [jax.ShapeDtypeStruct((nrows, kw), BF16)] * 2
    if rope_tabs is not None:
        per = DEC_SEQ // tm
        in_specs += [pl.BlockSpec((tm, HEAD_DIM), lambda i: (i % per, 0))] * 2
        args += list(rope_tabs)
    else:
        out_specs = [o_spec, o_spec] + out_specs
        out_shape = [jax.ShapeDtypeStruct((nrows, kw), F32)] * 2 + out_shape
    return pl.pallas_call(
        functools.partial(_kvprep_kernel, rope=rope_tabs is not None),
        grid=(nrows // tm,),
        in_specs=in_specs,
        out_specs=out_specs,
        out_shape=out_shape,
        compiler_params=_cparams(("parallel",), 16 * tm * kw * 4),
        name="kv_prep",
    )(*args)


def _attn_kernel(*refs, rope):
    if rope:
        q_ref, k_ref, v_ref, w_ref, cos_ref, sin_ref, o_ref = refs
    else:
        q_ref, k_ref, v_ref, w_ref, o_ref = refs
    kk = k_ref[...]
    vv = v_ref[...]
    scale = HEAD_DIM ** -0.5 * math.log2(math.e)
    for g in range(GQA_GROUP):
        cols = slice(g * HEAD_DIM, (g + 1) * HEAD_DIM)
        q = _head_norm(q_ref[:, cols], w_ref[...])
        if rope:
            q = _rope(q, cos_ref[...], sin_ref[...])
        s = _dot_nt((q * scale).astype(BF16), kk)
        p = jnp.exp2(s - jnp.max(s, axis=-1, keepdims=True))
        den = jnp.sum(p, axis=-1, keepdims=True)
        o = _dot(p.astype(BF16), vv) / den
        o_ref[:, cols] = o.astype(o_ref.dtype)


def _attention(qkv, kb, vb, q_norm, rows0, seq, nb, tk, rope_tabs):
    tq = 256
    gw = GQA_GROUP * HEAD_DIM
    nq = seq // tq
    rb0 = rows0 // tq
    in_specs = [pl.BlockSpec((tq, gw), lambda b, h, i: (rb0 + b * nq + i, h)),
                pl.BlockSpec((tk, HEAD_DIM), lambda b, h, i: (b, h)),
                pl.BlockSpec((tk, HEAD_DIM), lambda b, h, i: (b, h)),
                pl.BlockSpec((1, HEAD_DIM), lambda b, h, i: (0, 0))]
    args = [qkv, kb, vb, q_norm.reshape(1, HEAD_DIM)]
    if rope_tabs is not None:
        in_specs += [pl.BlockSpec((tq, HEAD_DIM), lambda b, h, i: (i, 0))] * 2
        args += list(rope_tabs)
    vm = 2 * (tq * gw * 4 + 2 * tk * HEAD_DIM * 2 + tq * gw * 2) + 4 * tq * tk * 4 + (4 << 20)
    return pl.pallas_call(
        functools.partial(_attn_kernel, rope=rope_tabs is not None),
        grid=(nb, N_KV_HEADS, nq),
        in_specs=in_specs,
        out_specs=pl.BlockSpec((tq, gw), lambda b, h, i: (b * nq + i, h)),
        out_shape=jax.ShapeDtypeStruct((nb * seq, D_MODEL), BF16),
        compiler_params=_cparams(("parallel", "parallel", "parallel"), vm),
        name="attention",
    )(*args)


def _rope_tables():
    n_rows = DEC_SEQ // GRID_W
    row = jnp.repeat(jnp.arange(n_rows), GRID_W).astype(F32)
    colp = jnp.tile(jnp.arange(GRID_W), n_rows).astype(F32)
    inv = ROPE_THETA ** (-jnp.arange(0, ROPE_AXIS_DIM, 2, dtype=F32) / ROPE_AXIS_DIM)
    ang_r = row[:, None] * inv
    ang_c = colp[:, None] * inv
    ang = jnp.concatenate([ang_r, ang_r, ang_c, ang_c], axis=-1)
    quarter = ROPE_AXIS_DIM // 2
    sign = jnp.where((jnp.arange(HEAD_DIM) % ROPE_AXIS_DIM) < quarter, -1.0, 1.0).astype(F32)
    return jnp.cos(ang), jnp.sin(ang) * sign


def _attn_mixer(hbf, w_qkv, w_out, layer, q_norm, k_norm, cache_k, cache_v, x, gate):
    qkv = _matmul(hbf, w_qkv, layer, _MM_TM, _MM_TN)
    kw = N_KV_HEADS * HEAD_DIM
    tabs = _rope_tables()
    kn_ctx, v_ctx, kb_ctx, vb_ctx = _kv_prep(qkv, k_norm, 0, N_CTX, None)
    kb_lat, vb_lat = _kv_prep(qkv, k_norm, N_CTX, N_LAT, tabs)
    o_ctx = _attention(qkv, kb_ctx, vb_ctx, q_norm, 0, SEQ, BATCH, SEQ, None)
    tk = PAST_LEN + DEC_SEQ
    keys = jnp.concatenate([cache_k.reshape(DEC_BATCH, PAST_LEN, kw).astype(BF16),
                            kb_lat.reshape(DEC_BATCH, DEC_SEQ, kw)], axis=1).reshape(DEC_BATCH * tk, kw)
    vals = jnp.concatenate([cache_v.reshape(DEC_BATCH, PAST_LEN, kw).astype(BF16),
                            vb_lat.reshape(DEC_BATCH, DEC_SEQ, kw)], axis=1).reshape(DEC_BATCH * tk, kw)
    o_lat = _attention(qkv, keys, vals, q_norm, N_CTX, DEC_SEQ, DEC_BATCH, tk, tabs)
    o = jnp.concatenate([o_ctx, o_lat], axis=0)
    new_k = kn_ctx.reshape(BATCH, SEQ, N_KV_HEADS, HEAD_DIM)
    new_v = v_ctx.reshape(BATCH, SEQ, N_KV_HEADS, HEAD_DIM)
    return _matmul_residual(o, w_out, layer, x, gate, _MM_TM, _MM_TN), new_k, new_v


def kernel(x_prompt, x_sample, state_hgrn, cache_k, cache_v, c, c_ctx, ada_w, ada_b, norm_w, final_norm_w,
           hgrn_w_in, hgrn_lb_logits, hgrn_norm_w, hgrn_w_out, conv_w_in, conv_w, conv_w_out,
           attn_w_qkv, attn_q_norm, attn_k_norm, attn_w_out, ffn_w_gate_up, ffn_w_down,
           moe_w_router, moe_w_gate_up, moe_w_down):
    x = jnp.concatenate([x_prompt.reshape(N_CTX, D_MODEL), x_sample.reshape(N_LAT, D_MODEL)], axis=0)
    conds = jnp.zeros((COND_PAD, D_MODEL), F32).at[0].set(c_ctx).at[1:N_COND].set(c)
    mod = _modulation(conds, ada_w, ada_b)
    mod = mod.reshape(DEPTH, COND_PAD, 6, 1, D_MODEL).transpose(0, 2, 1, 3, 4)

    probs = jax.nn.softmax(hgrn_lb_logits.astype(F32), axis=0)
    csum = jnp.cumsum(probs, axis=0)
    lower_bounds = csum - csum[:1]

    ffn_gu, ffn_down = ffn_w_gate_up.astype(BF16), ffn_w_down.astype(BF16)
    moe_gu, moe_down = moe_w_gate_up.astype(BF16), moe_w_down.astype(BF16)
    hgrn_w_in, hgrn_w_out = hgrn_w_in.astype(BF16), hgrn_w_out.astype(BF16)
    conv_w_in, conv_w_out = conv_w_in.astype(BF16), conv_w_out.astype(BF16)
    attn_w_qkv, attn_w_out = attn_w_qkv.astype(BF16), attn_w_out.astype(BF16)

    new_states, new_k, new_v = [], None, None
    for layer in range(DEPTH):
        kind = layer % N_MIXERS
        j = layer // N_MIXERS
        sh1, sc1, g1, sh2, sc2, g2 = (mod[layer, t] for t in range(6))
        hbf = _norm_mod(x, norm_w[layer, 0], sc1, sh1)
        if kind == 0:
            x, st = _hgrn_mixer(hbf, hgrn_w_in, hgrn_w_out, j, lower_bounds[j], hgrn_norm_w[j],
                                state_hgrn[:, j], x, g1)
            new_states.append(st)
        elif kind == 1:
            x = _conv_mixer(hbf, conv_w_in, conv_w_out, j, conv_w[j], x, g1)
        else:
            x, new_k, new_v = _attn_mixer(hbf, attn_w_qkv, attn_w_out, j, attn_q_norm[j], attn_k_norm[j],
                                          cache_k[:, j], cache_v[:, j], x, g1)
        f = layer // 2
        if layer % 2 == 0:
            hbf = _norm_mod(x, norm_w[layer, 1], sc2, sh2)
            x = _dense_ffn(hbf, ffn_gu, ffn_down, f, x, g2)
        else:
            x = _moe_ffn(x, norm_w[layer, 1], sc2, sh2, g2, moe_w_router[f], moe_gu, moe_down, f)
    y_prompt = _final_norm(x, final_norm_w, 0, N_CTX).reshape(BATCH, SEQ, D_MODEL)
    y_sample = _final_norm(x, final_norm_w, N_CTX, N_LAT).reshape(DEC_BATCH, DEC_SEQ, D_MODEL)
    new_state_hgrn = jnp.stack(new_states, axis=1)
    return (y_prompt, y_sample, new_state_hgrn, new_k[:, None], new_v[:, None])
```
